```python
import math
import jax, jax.numpy as jnp
from jax import lax
import numpy as np

D_MODEL = 1024
BATCH = 4
SEQ = 4096
DEPTH = 4
DEC_BATCH = 128
DEC_SEQ = 8
PAST_LEN = 2048
PAGE_SIZE = 128

N_MIXERS = 2
HEAD_DIM = 64
N_SELF_HEADS = 12
N_MEM_HEADS = 4
SELF_WIDTH = N_SELF_HEADS * HEAD_DIM
MEM_WIDTH = N_MEM_HEADS * HEAD_DIM
MIX_WIDTH = SELF_WIDTH + MEM_WIDTH
N_MEM = 256
NSA_KV_HEADS = 4
NSA_GROUP = N_SELF_HEADS // NSA_KV_HEADS
NSA_KV_WIDTH = NSA_KV_HEADS * HEAD_DIM
CMP_BLOCK = 32
SEL_BLOCK = 64
SEL_TOPK = 16
WINDOW = 512
Q_BLOCK = 128
ROPE_THETA = 500000.0
ROT_DIM = HEAD_DIM // 4
D_FF = -(-8 * D_MODEL // (3 * 256)) * 256
N_A_LAYERS = (DEPTH + 1) // 2
N_B_LAYERS = DEPTH // 2
SB_IN = 3 * SELF_WIDTH + MEM_WIDTH
NSA_IN = SELF_WIDTH + 6 * NSA_KV_WIDTH + 3 * N_SELF_HEADS + MEM_WIDTH
EPS = 1e-6
NEG = -1e30
SCALE = HEAD_DIM ** -0.5

kernel_name = 'stickbreak_nsa_memory_hybrid_step'


def rmsnorm(x, g):
    xf = x.astype(jnp.float32)
    y = xf * lax.rsqrt(jnp.mean(xf * xf, axis=-1, keepdims=True) + EPS)
    return (y * g.astype(jnp.float32)).astype(x.dtype)


def split_cols(x, sizes):
    return jnp.split(x, np.cumsum(sizes)[:-1].tolist(), axis=-1)


def partial_rope(x, pos):
    half = ROT_DIM // 2
    inv = ROPE_THETA ** (-jnp.arange(half, dtype=jnp.float32) * 2.0 / ROT_DIM)
    ang = pos.astype(jnp.float32)[:, None] * inv[None, :]
    cos = jnp.cos(ang)[None, :, None, :]
    sin = jnp.sin(ang)[None, :, None, :]
    xf = x.astype(jnp.float32)
    x1, x2, rest = xf[..., :half], xf[..., half:ROT_DIM], xf[..., ROT_DIM:]
    return jnp.concatenate([x1 * cos - x2 * sin, x2 * cos + x1 * sin, rest], axis=-1).astype(x.dtype)


def gather_pages(pool, page_table):
    g = pool[page_table]
    return g.reshape(g.shape[0], -1, *pool.shape[2:])


def query_blocks(fn, xs, t0, tq):
    qb = Q_BLOCK if tq % Q_BLOCK == 0 else tq
    nb = tq // qb
    split = lambda z: jnp.moveaxis(z.reshape(z.shape[0], nb, qb, *z.shape[2:]), 1, 0)
    starts = t0 + jnp.arange(nb, dtype=jnp.int32) * qb
    out = lax.map(lambda c: fn(c[0], c[1] + jnp.arange(qb, dtype=jnp.int32)),
                  (tuple(split(z) for z in xs), starts))
    out = jnp.moveaxis(out, 0, 1)
    return out.reshape(out.shape[0], tq, *out.shape[3:])


def mem_attend(qm, mk, mv):
    s = jnp.einsum('bqhd,bmhd->bhqm', qm, mk).astype(jnp.float32) * SCALE
    p = jax.nn.softmax(s, axis=-1)
    return jnp.einsum('bhqm,bmhd->bqhd', p, mv).astype(qm.dtype)


def memory_kv(mem, g, w):
    b = mem.shape[0]
    mk, mv = jnp.split(rmsnorm(mem, g) @ w, 2, axis=-1)
    return (mk.reshape(b, N_MEM, N_MEM_HEADS, HEAD_DIM), mv.reshape(b, N_MEM, N_MEM_HEADS, HEAD_DIM))


def sb_block(q, k, v, q_pos):
    tk = k.shape[1]
    z = jnp.einsum('bqhd,bkhd->bhqk', q, k).astype(jnp.float32) * SCALE
    mask = jnp.arange(tk)[None, :] < q_pos[:, None]
    log_keep = jnp.where(mask, jax.nn.log_sigmoid(-z), 0.0)
    after = lax.cumsum(log_keep, axis=3, reverse=True) - log_keep
    a = jnp.where(mask, jnp.exp(jax.nn.log_sigmoid(z) + after), 0.0)
    return jnp.einsum('bhqk,bkhd->bqhd', a, v).astype(q.dtype)


def sb_mixer(a, t0, w_in, mk, mv, past_k=None, past_v=None):
    b, t, _ = a.shape
    q, k, v, qm = split_cols(a @ w_in, [SELF_WIDTH, SELF_WIDTH, SELF_WIDTH, MEM_WIDTH])
    hq = lambda z: z.reshape(b, t, N_SELF_HEADS, HEAD_DIM)
    q, k, v = hq(q), hq(k), hq(v)
    kf = k if past_k is None else jnp.concatenate([past_k, k], axis=1)
    vf = v if past_v is None else jnp.concatenate([past_v, v], axis=1)
    o = query_blocks(lambda xs, qp: sb_block(xs[0], kf, vf, qp), (q,), t0, t)
    om = mem_attend(qm.reshape(b, t, N_MEM_HEADS, HEAD_DIM), mk, mv)
    y = jnp.concatenate([o.reshape(b, t, SELF_WIDTH), om.reshape(b, t, MEM_WIDTH)], axis=-1)
    return y, k, v


def compress(x, w, pe):
    b, tk = x.shape[:2]
    n = tk // CMP_BLOCK
    xb = x[:, :n * CMP_BLOCK].reshape(b, n, CMP_BLOCK, NSA_KV_HEADS, HEAD_DIM) + pe[None, None, :, None, :]
    return jnp.einsum('bnlgd,lde->bnge', xb, w)


def to_sel_blocks(x, n_sel):
    b, tk = x.shape[:2]
    xp = jnp.pad(x, ((0, 0), (0, n_sel * SEL_BLOCK - tk), (0, 0), (0, 0)))
    return xp.reshape(b, n_sel, SEL_BLOCK, NSA_KV_HEADS, HEAD_DIM).transpose(0, 3, 1, 2, 4)


def nsa_block(qr, qn, gate, q_pos, k_cmp, v_cmp, ks_blk, vs_blk, kw_pad, vw_pad, w0):
    b, nq = qr.shape[:2]
    qr_g = qr.reshape(b, nq, NSA_KV_HEADS, NSA_GROUP, HEAD_DIM)
    qn_g = qn.reshape(b, nq, NSA_KV_HEADS, NSA_GROUP, HEAD_DIM)
    n_cmp = k_cmp.shape[1]
    s_c = jnp.einsum('bqgrd,bngd->bgrqn', qn_g, k_cmp).astype(jnp.float32) * SCALE
    c_mask = (jnp.arange(n_cmp) * CMP_BLOCK + (CMP_BLOCK - 1))[None, :] <= q_pos[:, None]
    p_c = jax.nn.softmax(jnp.where(c_mask, s_c, NEG), axis=-1) * c_mask
    o_c = jnp.einsum('bgrqn,bngd->bqgrd', p_c, v_cmp)
    n_sel = ks_blk.shape[2]
    ratio = SEL_BLOCK // CMP_BLOCK
    imp = jnp.pad(p_c.sum(axis=2), ((0, 0), (0, 0), (0, 0), (0, n_sel * ratio - n_cmp)))
    imp = imp.reshape(b, NSA_KV_HEADS, nq, n_sel, ratio).sum(-1)
    cur = q_pos // SEL_BLOCK
    blk = jnp.arange(n_sel)[None, :]
    forced = (blk == 0) | (blk == cur[:, None]) | (blk == cur[:, None] - 1)
    allowed = blk <= cur[:, None]
    score = jnp.where(forced, jnp.inf, jnp.where(allowed, imp, -jnp.inf))
    n_top = min(SEL_TOPK, n_sel)
    _, idx = lax.top_k(score, n_top)
    bi = jnp.arange(b)[:, None, None, None]
    gi = jnp.arange(NSA_KV_HEADS)[None, :, None, None]
    k_sel = ks_blk[bi, gi, idx]
    v_sel = vs_blk[bi, gi, idx]
    s_s = jnp.einsum('bqgrd,bgqnkd->bgrqnk', qr_g, k_sel).astype(jnp.float32) * SCALE
    k_pos = idx[..., None] * SEL_BLOCK + jnp.arange(SEL_BLOCK)
    s_mask = ((idx <= cur[None, None, :, None])[..., None] & (k_pos <= q_pos[None, None, :, None, None]))[:, :, None]
    s_s = jnp.where(s_mask, s_s, NEG).reshape(b, NSA_KV_HEADS, NSA_GROUP, nq, n_top * SEL_BLOCK)
    p_s = jax.nn.softmax(s_s, axis=-1).reshape(b, NSA_KV_HEADS, NSA_GROUP, nq, n_top, SEL_BLOCK)
    o_s = jnp.einsum('bgrqnk,bgqnkd->bqgrd', p_s, v_sel)
    span = WINDOW - 1 + nq
    q0 = q_pos[0]
    k_w = lax.dynamic_slice_in_dim(kw_pad, q0 - w0, span, axis=1)
    v_w = lax.dynamic_slice_in_dim(vw_pad, q0 - w0, span, axis=1)
    w_pos = q0 - (WINDOW - 1) + jnp.arange(span)
    w_mask = ((w_pos[None, :] <= q_pos[:, None]) & (w_pos[None, :] > q_pos[:, None] - WINDOW)
              & (w_pos[None, :] >= w0))
    s_w = jnp.einsum('bqgrd,bkgd->bgrqk', qr_g, k_w).astype(jnp.float32) * SCALE
    p_w = jax.nn.softmax(jnp.where(w_mask, s_w, NEG), axis=-1)
    o_w = jnp.einsum('bgrqk,bkgd->bqgrd', p_w, v_w)
    g = jax.nn.sigmoid(gate.astype(jnp.float32))
    hs = (b, nq, N_SELF_HEADS, HEAD_DIM)
    o = g[..., 0:1] * o_c.reshape(hs) + g[..., 1:2] * o_s.reshape(hs) + g[..., 2:3] * o_w.reshape(hs)
    return o.astype(qr.dtype)


def nsa_mixer(a, t0, w_in, w_ck, w_cv, pe_ck, pe_cv, mk, mv, past=None):
    b, t, _ = a.shape
    q, kc, vc, ks, vs, kw, vw, gl, qm = split_cols(
        a @ w_in, [SELF_WIDTH] + [NSA_KV_WIDTH] * 6 + [3 * N_SELF_HEADS, MEM_WIDTH])
    hk = lambda z: z.reshape(b, t, NSA_KV_HEADS, HEAD_DIM)
    pos = t0 + jnp.arange(t, dtype=jnp.int32)
    qn = q.reshape(b, t, N_SELF_HEADS, HEAD_DIM)
    qr = partial_rope(qn, pos)
    new_rows = (hk(kc), hk(vc), partial_rope(hk(ks), pos), hk(vs), partial_rope(hk(kw), pos), hk(vw))
    if past is None:
        full = new_rows
        w0 = t0
    else:
        full = tuple(jnp.concatenate([p, n], axis=1) for p, n in zip(past, new_rows))
        w0 = t0 - past[4].shape[1]
    fkc, fvc, fks, fvs, fkw, fvw = full
    k_cmp = compress(fkc, w_ck, pe_ck)
    v_cmp = compress(fvc, w_cv, pe_cv)
    n_sel = -(-fks.shape[1] // SEL_BLOCK)
    ks_blk = to_sel_blocks(fks, n_sel)
    vs_blk = to_sel_blocks(fvs, n_sel)
    wpad = ((0, 0), (WINDOW - 1, 0), (0, 0), (0, 0))
    kw_pad = jnp.pad(fkw, wpad)
    vw_pad = jnp.pad(fvw, wpad)
    gate = gl.reshape(b, t, N_SELF_HEADS, 3)
    o = query_blocks(lambda xs, qp: nsa_block(xs[0], xs[1], xs[2], qp, k_cmp, v_cmp, ks_blk, vs_blk,
                                              kw_pad, vw_pad, w0), (qr, qn, gate), t0, t)
    om = mem_attend(qm.reshape(b, t, N_MEM_HEADS, HEAD_DIM), mk, mv)
    y = jnp.concatenate([o.reshape(b, t, SELF_WIDTH), om.reshape(b, t, MEM_WIDTH)], axis=-1)
    return y, new_rows


def layer_tail(h, y, w_out, g_mix_post, g_ffn_pre, g_ffn_post, w_gate_up, w_down):
    h = h + rmsnorm(y @ w_out, g_mix_post)
    gt, up = jnp.split(rmsnorm(h, g_ffn_pre) @ w_gate_up, 2, axis=-1)
    return h + rmsnorm((jax.nn.silu(gt) * up) @ w_down, g_ffn_post)


def setup_inputs(seed: int = 0) -> dict:
    key = jax.random.key(seed)
    ks = jax.random.split(key, 32)
    n_pages = PAST_LEN // PAGE_SIZE
    n_used = DEC_BATCH * n_pages
    n_pool = n_used + n_used // 4
    win_buf = min(WINDOW, PAST_LEN)
    nrm = lambda k, shape, s=1.0: jax.random.normal(k, shape, jnp.float32) * s
    gain = lambda k: 1.0 + nrm(k, (DEPTH, D_MODEL), 0.05)
    page_table = jax.random.permutation(ks[0], n_pool)[:n_used].reshape(DEC_BATCH, n_pages).astype(jnp.int32)
    sb_pool = (N_A_LAYERS, n_pool, PAGE_SIZE, N_SELF_HEADS, HEAD_DIM)
    nsa_pool = (N_B_LAYERS, n_pool, PAGE_SIZE, NSA_KV_HEADS, HEAD_DIM)
    win_shape = (N_B_LAYERS, DEC_BATCH, win_buf, NSA_KV_HEADS, HEAD_DIM)
    mem_shape = (DEPTH, DEC_BATCH, N_MEM, N_MEM_HEADS, HEAD_DIM)
    return {
        'x_prompt': nrm(ks[1], (BATCH, SEQ, D_MODEL)),
        'x_sample': nrm(ks[2], (DEC_BATCH, DEC_SEQ, D_MODEL)),
        'mem_prompt': nrm(ks[3], (BATCH, N_MEM, D_MODEL)),
        'cache_sb_k': nrm(ks[4], sb_pool),
        'cache_sb_v': nrm(ks[5], sb_pool),
        'cache_nsa_cmp_k': nrm(ks[6], nsa_pool),
        'cache_nsa_cmp_v': nrm(ks[7], nsa_pool),
        'cache_nsa_sel_k': nrm(ks[8], nsa_pool),
        'cache_nsa_sel_v': nrm(ks[9], nsa_pool),
        'cache_nsa_win_k': nrm(ks[10], win_shape),
        'cache_nsa_win_v': nrm(ks[11], win_shape),
        'cache_mem_k': nrm(ks[12], mem_shape),
        'cache_mem_v': nrm(ks[13], mem_shape),
        'page_table': page_table,
        'ln_mix_pre': gain(ks[14]),
        'ln_mix_post': gain(ks[15]),
        'ln_ffn_pre': gain(ks[16]),
        'ln_ffn_post': gain(ks[17]),
        'ln_mem': gain(ks[18]),
        'w_in_a': nrm(ks[19], (N_A_LAYERS, D_MODEL, SB_IN), D_MODEL ** -0.5),
        'w_in_b': nrm(ks[20], (N_B_LAYERS, D_MODEL, NSA_IN), D_MODEL ** -0.5),
        'w_cmp_k': nrm(ks[21], (N_B_LAYERS, CMP_BLOCK, HEAD_DIM, HEAD_DIM), (CMP_BLOCK * HEAD_DIM) ** -0.5),
        'w_cmp_v': nrm(ks[22], (N_B_LAYERS, CMP_BLOCK, HEAD_DIM, HEAD_DIM), (CMP_BLOCK * HEAD_DIM) ** -0.5),
        'pe_cmp_k': nrm(ks[23], (N_B_LAYERS, CMP_BLOCK, HEAD_DIM), 0.1),
        'pe_cmp_v': nrm(ks[24], (N_B_LAYERS, CMP_BLOCK, HEAD_DIM), 0.1),
        'w_out': nrm(ks[25], (DEPTH, MIX_WIDTH, D_MODEL), MIX_WIDTH ** -0.5),
        'w_mem_kv': nrm(ks[26], (DEPTH, D_MODEL, 2 * MEM_WIDTH), D_MODEL ** -0.5),
        'w_gate_up': nrm(ks[27], (DEPTH, D_MODEL, 2 * D_FF), D_MODEL ** -0.5),
        'w_down': nrm(ks[28], (DEPTH, D_FF, D_MODEL), D_FF ** -0.5),
    }


def reference(x_prompt, x_sample, mem_prompt, cache_sb_k, cache_sb_v, cache_nsa_cmp_k, cache_nsa_cmp_v,
              cache_nsa_sel_k, cache_nsa_sel_v, cache_nsa_win_k, cache_nsa_win_v, cache_mem_k, cache_mem_v,
              page_table, ln_mix_pre, ln_mix_post, ln_ffn_pre, ln_ffn_post, ln_mem, w_in_a, w_in_b,
              w_cmp_k, w_cmp_v, pe_cmp_k, pe_cmp_v, w_out, w_mem_kv, w_gate_up, w_down):
    past = page_table.shape[1] * PAGE_SIZE
    win_keep = min(WINDOW, x_prompt.shape[1])
    h_p, h_s = x_prompt, x_sample
    sbk_p, sbv_p, sbk_s, sbv_s = [], [], [], []
    nsa_p = [[] for _ in range(6)]
    nsa_s = [[] for _ in range(6)]
    memk_p, memv_p = [], []
    for i in range(DEPTH):
        j = i // N_MIXERS
        mk_p, mv_p = memory_kv(mem_prompt, ln_mem[i], w_mem_kv[i])
        memk_p.append(mk_p)
        memv_p.append(mv_p)
        a_p = rmsnorm(h_p, ln_mix_pre[i])
        a_s = rmsnorm(h_s, ln_mix_pre[i])
        if i % N_MIXERS == 0:
            y_p, k_p, v_p = sb_mixer(a_p, 0, w_in_a[j], mk_p, mv_p)
            y_s, k_s, v_s = sb_mixer(a_s, past, w_in_a[j], cache_mem_k[i], cache_mem_v[i],
                                     gather_pages(cache_sb_k[j], page_table),
                                     gather_pages(cache_sb_v[j], page_table))
            sbk_p.append(k_p)
            sbv_p.append(v_p)
            sbk_s.append(k_s)
            sbv_s.append(v_s)
        else:
            y_p, rows_p = nsa_mixer(a_p, 0, w_in_b[j], w_cmp_k[j], w_cmp_v[j], pe_cmp_k[j], pe_cmp_v[j],
                                    mk_p, mv_p)
            past_rows = (gather_pages(cache_nsa_cmp_k[j], page_table), gather_pages(cache_nsa_cmp_v[j], page_table),
                         gather_pages(cache_nsa_sel_k[j], page_table), gather_pages(cache_nsa_sel_v[j], page_table),
                         cache_nsa_win_k[j], cache_nsa_win_v[j])
            y_s, rows_s = nsa_mixer(a_s, past, w_in_b[j], w_cmp_k[j], w_cmp_v[j], pe_cmp_k[j], pe_cmp_v[j],
                                    cache_mem_k[i], cache_mem_v[i], past_rows)
            for n in range(6):
                nsa_p[n].append(rows_p[n][:, -win_keep:] if n >= 4 else rows_p[n])
                nsa_s[n].append(rows_s[n])
        h_p = layer_tail(h_p, y_p, w_out[i], ln_mix_post[i], ln_ffn_pre[i], ln_ffn_post[i], w_gate_up[i], w_down[i])
        h_s = layer_tail(h_s, y_s, w_out[i], ln_mix_post[i], ln_ffn_pre[i], ln_ffn_post[i], w_gate_up[i], w_down[i])
    st = lambda rows: jnp.stack(rows, axis=0)
    return (h_p, h_s,
            st(sbk_p), st(sbv_p),
            st(nsa_p[0]), st(nsa_p[1]), st(nsa_p[2]), st(nsa_p[3]), st(nsa_p[4]), st(nsa_p[5]),
            st(memk_p), st(memv_p),
            st(sbk_s), st(sbv_s),
            st(nsa_s[0]), st(nsa_s[1]), st(nsa_s[2]), st(nsa_s[3]), st(nsa_s[4]), st(nsa_s[5]))
```

```python
import functools

import jax
import jax.numpy as jnp
import numpy as np
from jax import lax
from jax.experimental import pallas as pl
from jax.experimental.pallas import tpu as pltpu

F32 = jnp.float32
BF = jnp.bfloat16

D_MODEL = 1024
HEAD_DIM = 64
N_SELF_HEADS = 12
N_MEM_HEADS = 4
SELF_WIDTH = N_SELF_HEADS * HEAD_DIM
MEM_WIDTH = N_MEM_HEADS * HEAD_DIM
N_MEM = 256
NSA_KV_HEADS = 4
NSA_GROUP = N_SELF_HEADS // NSA_KV_HEADS
KV_WIDTH = NSA_KV_HEADS * HEAD_DIM
CMP_BLOCK = 32
SEL_BLOCK = 64
SEL_TOPK = 16
WINDOW = 512
PAGE_SIZE = 128
ROPE_THETA = 500000.0
ROT_DIM = HEAD_DIM // 4
EPS = 1e-6
NEG = -1e30
SCALE = HEAD_DIM ** -0.5

LANES = 128
KEY_BLOCK = 256
VMEM_LIMIT = 56 * 1024 * 1024


def _cp(*sem):
    return pltpu.CompilerParams(dimension_semantics=sem, vmem_limit_bytes=VMEM_LIMIT)


def _dot(a, b):
    return jnp.dot(a, b, preferred_element_type=F32)


def _dot_nt(a, b):
    return lax.dot_general(a, b, (((1,), (1,)), ((), ())), preferred_element_type=F32)


def _dot_tn(a, b):
    return lax.dot_general(a, b, (((0,), (0,)), ((), ())), preferred_element_type=F32)


def _rms(x, g):
    return x * lax.rsqrt(jnp.mean(x * x, axis=-1, keepdims=True) + EPS) * g


def _sigmoid(x):
    return 1.0 / (1.0 + jnp.exp(-x))


def _head_of(idx):
    return jnp.right_shift(idx, 6)


def _const_spec(shape):
    nd = len(shape)
    return pl.BlockSpec(shape, lambda *_: (0,) * nd, pipeline_mode=pl.Buffered(1))


def _split_bf(x):
    hi = x.astype(BF)
    lo = (x - hi.astype(F32)).astype(BF)
    return hi, lo


def _rope128(x, c, s):
    lane = lax.broadcasted_iota(jnp.int32, x.shape, 1)
    up = pltpu.roll(x, LANES - ROT_DIM // 2, axis=1)
    dn = pltpu.roll(x, ROT_DIM // 2, axis=1)
    sw = jnp.where(jnp.bitwise_and(lane, HEAD_DIM - 1) < ROT_DIM // 2, up, dn)
    return x * c + sw * s


def _rope_tables(pos):
    half = ROT_DIM // 2
    inv = ROPE_THETA ** (-jnp.arange(half, dtype=F32) * 2.0 / ROT_DIM)
    ang = pos.astype(F32)[:, None] * inv[None, :]
    cos, sin = jnp.cos(ang), jnp.sin(ang)
    n = pos.shape[0]
    c64 = jnp.concatenate([cos, cos, jnp.ones((n, HEAD_DIM - ROT_DIM), F32)], axis=1)
    s64 = jnp.concatenate([-sin, sin, jnp.zeros((n, HEAD_DIM - ROT_DIM), F32)], axis=1)
    return jnp.tile(c64, (1, LANES // HEAD_DIM)), jnp.tile(s64, (1, LANES // HEAD_DIM))


def _sb_inproj_kernel(x_ref, g_ref, w_ref, wvt_ref, q_ref, k_ref, v_ref, kb_ref, qm_ref, *vt_refs):
    xn = _rms(x_ref[...], g_ref[...]).astype(BF)
    q_ref[...] = (_dot(xn, w_ref[:, 0:SELF_WIDTH]) * SCALE).astype(q_ref.dtype)
    k = _dot(xn, w_ref[:, SELF_WIDTH:2 * SELF_WIDTH])
    k_ref[...] = k
    kb_ref[...] = k.astype(BF)
    v_ref[...] = _dot(xn, w_ref[:, 2 * SELF_WIDTH:3 * SELF_WIDTH])
    qm_ref[...] = (_dot(xn, w_ref[:, 3 * SELF_WIDTH:3 * SELF_WIDTH + MEM_WIDTH]) * SCALE).astype(qm_ref.dtype)
    if vt_refs:
        (vt_ref,) = vt_refs
        vt = _dot_nt(wvt_ref[...], xn)
        for c in range(vt_ref.shape[0]):
            vt_ref[c] = vt[:, c * KEY_BLOCK:(c + 1) * KEY_BLOCK].astype(BF)


def _sb_inproj(x, g, w_bf, wvt_bf, *, tm, with_vt, q_dtype):
    m = x.shape[0]
    grid = (m // tm,)
    row = lambda i: (i, 0)
    const = lambda i: (0, 0)
    out_shape = [jax.ShapeDtypeStruct((m, SELF_WIDTH), q_dtype),
                 jax.ShapeDtypeStruct((m, SELF_WIDTH), F32),
                 jax.ShapeDtypeStruct((m, SELF_WIDTH), F32),
                 jax.ShapeDtypeStruct((m, SELF_WIDTH), BF),
                 jax.ShapeDtypeStruct((m, MEM_WIDTH), q_dtype)]
    out_specs = [pl.BlockSpec((tm, SELF_WIDTH), row)] * 4 + [pl.BlockSpec((tm, MEM_WIDTH), row)]
    if with_vt:
        out_shape.append(jax.ShapeDtypeStruct((m // KEY_BLOCK, SELF_WIDTH, KEY_BLOCK), BF))
        out_specs.append(pl.BlockSpec((tm // KEY_BLOCK, SELF_WIDTH, KEY_BLOCK), lambda i: (i, 0, 0)))
    return pl.pallas_call(
        _sb_inproj_kernel,
        grid=grid,
        in_specs=[pl.BlockSpec((tm, D_MODEL), row), _const_spec((1, D_MODEL)),
                  _const_spec(w_bf.shape), _const_spec(wvt_bf.shape)],
        out_specs=out_specs,
        out_shape=out_shape,
        compiler_params=_cp("parallel"),
        name="sb_inproj",
    )(x, g, w_bf, wvt_bf)


_NSA_Q0 = 0
_NSA_KV0 = SELF_WIDTH
_NSA_G0 = _NSA_KV0 + 6 * KV_WIDTH
_NSA_QM0 = _NSA_G0 + 2 * LANES
_NSA_COLS = _NSA_QM0 + MEM_WIDTH


def _nsa_inproj_kernel(x_ref, g_ref, w_ref, wt_ref, c_ref, s_ref,
                       qn0, qn1, qn2, qr0, qr1, qr2, kc_ref, vc_ref, ks_ref, vs_ref, kw_ref, vw_ref,
                       ksb_ref, kwb_ref, gate_ref, qm_ref, *vt_refs):
    xn = _rms(x_ref[...], g_ref[...]).astype(BF)
    c = c_ref[...]
    s = s_ref[...]

    def rope256(y):
        return jnp.concatenate([_rope128(y[:, :LANES], c, s), _rope128(y[:, LANES:], c, s)], axis=1)

    for r, (qn_ref, qr_ref) in enumerate(((qn0, qr0), (qn1, qr1), (qn2, qr2))):
        q = _dot(xn, w_ref[:, r * KV_WIDTH:(r + 1) * KV_WIDTH]) * SCALE
        qn_ref[...] = q.astype(qn_ref.dtype)
        qr_ref[...] = rope256(q).astype(qr_ref.dtype)

    def kv(j):
        return _dot(xn, w_ref[:, _NSA_KV0 + j * KV_WIDTH:_NSA_KV0 + (j + 1) * KV_WIDTH])

    kc_ref[...] = kv(0)
    vc_ref[...] = kv(1)
    ks = rope256(kv(2))
    ks_ref[...] = ks
    ksb_ref[...] = ks.astype(BF)
    vs_ref[...] = kv(3)
    kw = rope256(kv(4))
    kw_ref[...] = kw
    kwb_ref[...] = kw.astype(BF)
    vw_ref[...] = kv(5)
    gate_ref[...] = _dot(xn, w_ref[:, _NSA_G0:_NSA_G0 + 2 * LANES])
    qm_ref[...] = (_dot(xn, w_ref[:, _NSA_QM0:_NSA_QM0 + MEM_WIDTH]) * SCALE).astype(qm_ref.dtype)
    if vt_refs:
        vst_ref, vwt_ref = vt_refs
        vt = _dot_nt(wt_ref[...], xn)
        for cblk in range(vst_ref.shape[0]):
            sl = slice(cblk * KEY_BLOCK, (cblk + 1) * KEY_BLOCK)
            vst_ref[cblk] = vt[:KV_WIDTH, sl].astype(BF)
            vwt_ref[cblk] = vt[KV_WIDTH:, sl].astype(BF)


def _nsa_inproj(x, g, w_bf, wt_bf, ctab, stab, *, tm, with_vt, q_dtype):
    m = x.shape[0]
    grid = (m // tm,)
    row = lambda i: (i, 0)
    const = lambda i: (0, 0)
    tbl_blocks = ctab.shape[0] // tm
    tbl = lambda i: (i % tbl_blocks, 0)
    kvs = lambda dt: jax.ShapeDtypeStruct((m, KV_WIDTH), dt)
    out_shape = [kvs(q_dtype)] * 6 + [kvs(F32)] * 6 + [kvs(BF)] * 2 + [kvs(F32), kvs(q_dtype)]
    out_specs = [pl.BlockSpec((tm, KV_WIDTH), row)] * 16
    if with_vt:
        out_shape += [jax.ShapeDtypeStruct((m // KEY_BLOCK, KV_WIDTH, KEY_BLOCK), BF)] * 2
        out_specs += [pl.BlockSpec((tm // KEY_BLOCK, KV_WIDTH, KEY_BLOCK), lambda i: (i, 0, 0))] * 2
    return pl.pallas_call(
        _nsa_inproj_kernel,
        grid=grid,
        in_specs=[pl.BlockSpec((tm, D_MODEL), row), _const_spec((1, D_MODEL)),
                  _const_spec(w_bf.shape), _const_spec(wt_bf.shape),
                  pl.BlockSpec((tm, LANES), tbl), pl.BlockSpec((tm, LANES), tbl)],
        out_specs=out_specs,
        out_shape=out_shape,
        compiler_params=_cp("parallel"),
        name="nsa_inproj",
    )(x, g, w_bf, wt_bf, ctab, stab)


def _memkv_kernel(x_ref, g_ref, w_ref, wvt_ref, k_ref, v_ref, kb_ref, vt_ref):
    xn = _rms(x_ref[...], g_ref[...]).astype(BF)
    k = _dot(xn, w_ref[:, :MEM_WIDTH])
    k_ref[...] = k
    kb_ref[...] = k.astype(BF)
    v_ref[...] = _dot(xn, w_ref[:, MEM_WIDTH:])
    vt_ref[...] = _dot_nt(wvt_ref[...], xn).astype(BF)


def _memkv(x, g, w_bf, wvt_bf):
    m = x.shape[0]
    row = lambda i: (i, 0)
    const = lambda i: (0, 0)
    return pl.pallas_call(
        _memkv_kernel,
        grid=(m // N_MEM,),
        in_specs=[pl.BlockSpec((N_MEM, D_MODEL), row), _const_spec((1, D_MODEL)),
                  _const_spec(w_bf.shape), _const_spec(wvt_bf.shape)],
        out_specs=[pl.BlockSpec((N_MEM, MEM_WIDTH), row)] * 3 + [pl.BlockSpec((MEM_WIDTH, N_MEM), lambda i: (0, i))],
        out_shape=[jax.ShapeDtypeStruct((m, MEM_WIDTH), F32)] * 2 + [jax.ShapeDtypeStruct((m, MEM_WIDTH), BF),
                                                                     jax.ShapeDtypeStruct((MEM_WIDTH, m), BF)],
        compiler_params=_cp("parallel"),
        name="memkv",
    )(x, g, w_bf, wvt_bf)


def _tail_kernel(n_mix, *refs):
    h_ref = refs[0]
    mix_refs = refs[1:1 + n_mix]
    wo_ref, gpost_ref, gpre_ref, gfpost_ref, wgu_ref, wd_ref, o_ref = refs[1 + n_mix:]
    y = None
    off = 0
    for mref in mix_refs:
        w = mref.shape[1]
        part = _dot(mref[...].astype(BF), wo_ref[off:off + w, :])
        y = part if y is None else y + part
        off += w
    h = h_ref[...] + _rms(y, gpost_ref[...])
    xn = _rms(h, gpre_ref[...]).astype(BF)
    d_ff = wd_ref.shape[0]
    acc = None
    for c0 in range(0, d_ff, KEY_BLOCK):
        gt = _dot(xn, wgu_ref[:, c0:c0 + KEY_BLOCK])
        up = _dot(xn, wgu_ref[:, d_ff + c0:d_ff + c0 + KEY_BLOCK])
        act = (gt * _sigmoid(gt) * up).astype(BF)
        part = _dot(act, wd_ref[c0:c0 + KEY_BLOCK, :])
        acc = part if acc is None else acc + part
    o_ref[...] = h + _rms(acc, gfpost_ref[...])


def _layer_tail(h, mixes, wo_bf, g_post, g_pre, g_fpost, wgu_bf, wd_bf, *, tm):
    m = h.shape[0]
    row = lambda i: (i, 0)
    const = lambda i: (0, 0)
    return pl.pallas_call(
        functools.partial(_tail_kernel, len(mixes)),
        grid=(m // tm,),
        in_specs=[pl.BlockSpec((tm, D_MODEL), row)] + [pl.BlockSpec((tm, x.shape[1]), row) for x in mixes]
        + [_const_spec(wo_bf.shape)] + [_const_spec((1, D_MODEL))] * 3
        + [_const_spec(wgu_bf.shape), _const_spec(wd_bf.shape)],
        out_specs=pl.BlockSpec((tm, D_MODEL), row),
        out_shape=jax.ShapeDtypeStruct((m, D_MODEL), F32),
        compiler_params=_cp("parallel"),
        name="layer_tail",
    )(h, *mixes, wo_bf, g_post, g_pre, g_fpost, wgu_bf, wd_bf)


def _half_masks(shape, axis):
    idx = lax.broadcasted_iota(jnp.int32, shape, axis)
    return idx < HEAD_DIM, idx >= HEAD_DIM


def _sb_attn_kernel(q_ref, k_ref, vt_ref, u_ref, o_ref, acc_ref, carry_ref):
    tq = q_ref.shape[0]
    i = pl.program_id(2)
    qi = q_ref[...]
    lo_l, hi_l = _half_masks(qi.shape, 1)
    zero = jnp.zeros_like(qi)
    qh = (jnp.where(lo_l, qi, zero), jnp.where(hi_l, qi, zero))
    acc_ref[...] = jnp.zeros_like(acc_ref)
    carry_ref[...] = jnp.zeros_like(carry_ref)
    u = u_ref[...]

    def block(j, diag):
        kb = k_ref[pl.ds(pl.multiple_of(j * tq, tq), tq), :]
        vt = vt_ref[j]
        lo_r, hi_r = _half_masks(vt.shape, 0)
        zv = jnp.zeros_like(vt)
        vth = (jnp.where(lo_r, vt, zv), jnp.where(hi_r, vt, zv))
        if diag:
            kr = lax.broadcasted_iota(jnp.int32, (tq, tq), 0)
            qc = lax.broadcasted_iota(jnp.int32, (tq, tq), 1)
            mask = kr < qc
        contrib = None
        for h in range(2):
            z = _dot_nt(kb, qh[h])
            sp = jnp.maximum(z, 0.0) + jnp.log(1.0 + jnp.exp(-jnp.abs(z)))
            lk = -sp
            if diag:
                lk = jnp.where(mask, lk, 0.0)
            hi, lo = _split_bf(lk)
            after = _dot(u, hi) + _dot(u, lo)
            c = carry_ref[h:h + 1, :]
            a = jnp.exp((z - sp) + after + c)
            if diag:
                a = jnp.where(mask, a, 0.0)
            carry_ref[h:h + 1, :] = c + jnp.sum(lk, axis=0, keepdims=True)
            part = _dot(vth[h], a.astype(BF))
            contrib = part if contrib is None else contrib + part
        acc_ref[...] += contrib

    block(i, True)

    def body(jj, carry):
        block(i - 1 - jj, False)
        return carry

    lax.fori_loop(0, i, body, 0)
    o_ref[...] = acc_ref[...].T.astype(o_ref.dtype)


def _sb_attn(q_bf, k_bf, vt3, u_bf, *, batch, seq):
    m = q_bf.shape[0]
    tq = KEY_BLOCK
    nq = seq // tq
    return pl.pallas_call(
        _sb_attn_kernel,
        grid=(batch, SELF_WIDTH // LANES, nq),
        in_specs=[pl.BlockSpec((tq, LANES), lambda b, p, i: (b * nq + i, p)),
                  pl.BlockSpec((seq, LANES), lambda b, p, i: (b, p)),
                  pl.BlockSpec((nq, LANES, tq), lambda b, p, i: (b, p, 0)),
                  _const_spec((tq, tq))],
        out_specs=pl.BlockSpec((tq, LANES), lambda b, p, i: (b * nq + i, p)),
        out_shape=jax.ShapeDtypeStruct((m, SELF_WIDTH), BF),
        scratch_shapes=[pltpu.VMEM((LANES, tq), F32), pltpu.VMEM((8, tq), F32)],
        compiler_params=_cp("parallel", "parallel", "arbitrary"),
        name="sb_attn",
    )(q_bf, k_bf, vt3, u_bf)


def _mem_attn_kernel(q_ref, k_ref, vt_ref, o_ref):
    qi = q_ref[...]
    k = k_ref[...]
    vt = vt_ref[...]
    lane_h = _head_of(lax.broadcasted_iota(jnp.int32, qi.shape, 1))
    row_h = _head_of(lax.broadcasted_iota(jnp.int32, vt.shape, 0))
    acc = None
    for h in range(N_MEM_HEADS):
        s = _dot_nt(k, jnp.where(lane_h == h, qi, jnp.zeros_like(qi)))
        e = jnp.exp(s - jnp.max(s, axis=0, keepdims=True))
        p = e / jnp.sum(e, axis=0, keepdims=True)
        part = _dot(jnp.where(row_h == h, vt, jnp.zeros_like(vt)), p.astype(BF))
        acc = part if acc is None else acc + part
    o_ref[...] = acc.T.astype(o_ref.dtype)


def _mem_attn(qm_bf, mk_bf, mvt_bf, *, batch, seq):
    m = qm_bf.shape[0]
    tq = KEY_BLOCK
    nq = seq // tq
    return pl.pallas_call(
        _mem_attn_kernel,
        grid=(batch, nq),
        in_specs=[pl.BlockSpec((tq, MEM_WIDTH), lambda b, i: (b * nq + i, 0)),
                  pl.BlockSpec((N_MEM, MEM_WIDTH), lambda b, i: (b, 0)),
                  pl.BlockSpec((MEM_WIDTH, N_MEM), lambda b, i: (0, b))],
        out_specs=pl.BlockSpec((tq, MEM_WIDTH), lambda b, i: (b * nq + i, 0)),
        out_shape=jax.ShapeDtypeStruct((m, MEM_WIDTH), BF),
        compiler_params=_cp("parallel", "parallel"),
        name="mem_attn",
    )(qm_bf, mk_bf, mvt_bf)


def _compress_rows(x_ref, pe_ref, w_ref, nblk):
    acc = None
    for l in range(CMP_BLOCK):
        xs = [x_ref[pl.ds(2 * l + hf, nblk, stride=2 * CMP_BLOCK), :] for hf in range(2)]
        xl = jnp.concatenate(xs, axis=0) + pe_ref[l:l + 1, :]
        part = _dot(xl.astype(BF), w_ref[l])
        acc = part if acc is None else acc + part
    return jnp.concatenate([acc[:nblk], acc[nblk:]], axis=1)


def _compress_kernel(x_ref, pe_ref, w_ref, o_ref, ot_ref):
    out = _compress_rows(x_ref, pe_ref, w_ref, o_ref.shape[0])
    o_ref[...] = out.astype(o_ref.dtype)
    ot_ref[...] = out.T.astype(ot_ref.dtype)


def _compress(x2, pe2, wbd2, *, rows):
    m = x2.shape[0] // 2
    nblk = rows // CMP_BLOCK
    return pl.pallas_call(
        _compress_kernel,
        grid=(m // rows,),
        in_specs=[pl.BlockSpec((2 * rows, LANES), lambda i: (i, 0)),
                  _const_spec(pe2.shape), _const_spec(wbd2.shape)],
        out_specs=[pl.BlockSpec((nblk, KV_WIDTH), lambda i: (i, 0)), pl.BlockSpec((KV_WIDTH, nblk), lambda i: (0, i))],
        out_shape=[jax.ShapeDtypeStruct((m // CMP_BLOCK, KV_WIDTH), BF),
                   jax.ShapeDtypeStruct((KV_WIDTH, m // CMP_BLOCK), BF)],
        compiler_params=_cp("parallel"),
        name="compress",
    )(x2, pe2, wbd2)


def _topk_select(score_ref, n_blocks, cur):
    score = score_ref[...]
    blk = lax.broadcasted_iota(jnp.int32, score.shape, 0)

    def body(i, cnt):
        si = score_ref[pl.ds(i, 1), :]
        tie = jnp.where(blk > i, 1.0, 0.0)
        return cnt + jnp.where(si > score, 1.0, jnp.where(si == score, tie, 0.0))

    rank = lax.fori_loop(0, n_blocks, body, jnp.zeros(score.shape, F32))
    return jnp.where((rank < SEL_TOPK) & (blk <= cur), 1.0, 0.0).astype(F32)


def _nsa_attn_kernel(qn0, qn1, qn2, qr0, qr1, qr2, kc_ref, vct_ref, ks_ref, vst_ref, kw_ref, vwt_ref, gate_ref,
                     o0, o1, o2, imp_ref, score_ref, sel_ref, m_ref, l_ref, acc_ref, occ_ref):
    tq = qn0.shape[0]
    tk = tq
    i = pl.program_id(2)
    qn_refs = (qn0, qn1, qn2)
    qr_refs = (qr0, qr1, qr2)
    o_refs = (o0, o1, o2)
    qpos = i * tq + lax.broadcasted_iota(jnp.int32, (1, tq), 1)

    def halves(x):
        lo, hi = _half_masks(x.shape, 1)
        z = jnp.zeros_like(x)
        return jnp.where(lo, x, z), jnp.where(hi, x, z)

    def by_row_half(a, b):
        lo, _ = _half_masks((LANES, tq), 0)
        return jnp.where(lo, a, b)

    def vhalves(vt):
        lo, hi = _half_masks(vt.shape, 0)
        z = jnp.zeros_like(vt)
        return jnp.where(lo, vt, z), jnp.where(hi, vt, z)

    n_cmp = kc_ref.shape[0]
    kc = kc_ref[...]
    vct = vhalves(vct_ref[...])
    nrow = lax.broadcasted_iota(jnp.int32, (n_cmp, tq), 0)
    c_mask = (nrow * CMP_BLOCK + (CMP_BLOCK - 1)) <= qpos
    imp = [None, None]
    for r in range(NSA_GROUP):
        qh = halves(qn_refs[r][...])
        occ = None
        for hf in range(2):
            s = jnp.where(c_mask, _dot_nt(kc, qh[hf]), NEG)
            e = jnp.exp(s - jnp.max(s, axis=0, keepdims=True))
            p = jnp.where(c_mask, e / jnp.sum(e, axis=0, keepdims=True), 0.0)
            imp[hf] = p if imp[hf] is None else imp[hf] + p
            part = _dot(vct[hf], p.astype(BF))
            occ = part if occ is None else occ + part
        occ_ref[r] = occ

    n_sel = sel_ref.shape[1]
    ratio = SEL_BLOCK // CMP_BLOCK
    blk = lax.broadcasted_iota(jnp.int32, (n_sel, tq), 0)
    cur = _head_of(qpos)
    forced = (blk == 0) | (blk == cur) | (blk == cur - 1)
    allowed = blk <= cur
    scores = []
    for hf in range(2):
        parts = []
        for c in range(tq // LANES):
            imp_ref[c] = imp[hf][:, c * LANES:(c + 1) * LANES]
            parts.append(imp_ref[c, pl.ds(0, n_sel, stride=ratio), :] + imp_ref[c, pl.ds(1, n_sel, stride=ratio), :])
        blk_imp = jnp.concatenate(parts, axis=1)
        scores.append(jnp.where(forced, jnp.inf, jnp.where(allowed, blk_imp, -jnp.inf)))
    for hf in range(2):
        score_ref[...] = scores[hf]
        sel_ref[hf] = _topk_select(score_ref, n_sel, cur)

    def attend(q_halves, kb, vt, masks):
        vth = vhalves(vt)
        for r in range(NSA_GROUP):
            contrib = None
            alphas = []
            for hf in range(2):
                hd = 2 * r + hf
                s = jnp.where(masks[hf], _dot_nt(kb, q_halves[r][hf]), NEG)
                m_old = m_ref[hd:hd + 1, :]
                m_new = jnp.maximum(m_old, jnp.max(s, axis=0, keepdims=True))
                alpha = jnp.exp(m_old - m_new)
                p = jnp.exp(s - m_new)
                l_ref[hd:hd + 1, :] = alpha * l_ref[hd:hd + 1, :] + jnp.sum(p, axis=0, keepdims=True)
                m_ref[hd:hd + 1, :] = m_new
                part = _dot(vth[hf], p.astype(BF))
                contrib = part if contrib is None else contrib + part
                alphas.append(alpha)
            acc_ref[r] = acc_ref[r] * by_row_half(alphas[0], alphas[1]) + contrib

    def reset():
        m_ref[...] = jnp.full(m_ref.shape, NEG, F32)
        l_ref[...] = jnp.zeros_like(l_ref)
        acc_ref[...] = jnp.zeros_like(acc_ref)

    def finish():
        outs = []
        for r in range(NSA_GROUP):
            inv = by_row_half(1.0 / l_ref[2 * r:2 * r + 1, :], 1.0 / l_ref[2 * r + 1:2 * r + 2, :])
            outs.append(acc_ref[r] * inv)
        return outs

    qr_h = [halves(qr_refs[r][...]) for r in range(NSA_GROUP)]
    krow = lax.broadcasted_iota(jnp.int32, (tk, tq), 0)
    n_sub = tk // SEL_BLOCK

    reset()

    def sel_block(j, causal):
        kb = ks_ref[pl.ds(pl.multiple_of(j * tk, tk), tk), :]
        masks = []
        for hf in range(2):
            selv = jnp.concatenate(
                [jnp.broadcast_to(sel_ref[hf, pl.ds(j * n_sub + u, 1), :], (SEL_BLOCK, tq)) for u in range(n_sub)],
                axis=0)
            mk = selv > 0.5
            if causal:
                mk = mk & ((j * tk + krow) <= qpos)
            masks.append(mk)
        attend(qr_h, kb, vst_ref[j], masks)

    def sel_body(j, carry):
        sel_block(j, False)
        return carry

    lax.fori_loop(0, i, sel_body, 0)
    sel_block(i, True)
    o_sel = finish()

    reset()

    def win_body(j, carry):
        kpos = j * tk + krow
        mk = (kpos <= qpos) & (kpos > qpos - WINDOW)
        attend(qr_h, kw_ref[pl.ds(pl.multiple_of(j * tk, tk), tk), :], vwt_ref[j], (mk, mk))
        return carry

    lax.fori_loop(jnp.maximum(i - WINDOW // tk, 0), i + 1, win_body, 0)
    o_win = finish()

    gt = _sigmoid(gate_ref[...]).T
    for r in range(NSA_GROUP):
        def gate(branch):
            c = branch * 2 * NSA_GROUP + 2 * r
            return by_row_half(gt[c:c + 1, :], gt[c + 1:c + 2, :])
        o = gate(0) * occ_ref[r] + gate(1) * o_sel[r] + gate(2) * o_win[r]
        o_refs[r][...] = o.T.astype(o_refs[r].dtype)


def _nsa_attn(qn, qr, kc_bf, vct_bf, ks_bf, vst3, kw_bf, vwt3, gates, *, batch, seq):
    m = qn[0].shape[0]
    tq = KEY_BLOCK
    nq = seq // tq
    n_cmp = seq // CMP_BLOCK
    n_sel = seq // SEL_BLOCK
    qspec = pl.BlockSpec((tq, LANES), lambda b, p, i: (b * nq + i, p))
    kspec = pl.BlockSpec((seq, LANES), lambda b, p, i: (b, p))
    vspec = pl.BlockSpec((nq, LANES, tq), lambda b, p, i: (b, p, 0))
    return pl.pallas_call(
        _nsa_attn_kernel,
        grid=(batch, KV_WIDTH // LANES, nq),
        in_specs=[qspec] * 6 + [pl.BlockSpec((n_cmp, LANES), lambda b, p, i: (b, p)),
                                pl.BlockSpec((LANES, n_cmp), lambda b, p, i: (p, b)),
                                kspec, vspec, kspec, vspec, qspec],
        out_specs=[qspec] * 3,
        out_shape=[jax.ShapeDtypeStruct((m, KV_WIDTH), BF)] * 3,
        scratch_shapes=[pltpu.VMEM((tq // LANES, n_cmp, LANES), F32), pltpu.VMEM((n_sel, tq), F32),
                        pltpu.VMEM((2, n_sel, tq), F32), pltpu.VMEM((8, tq), F32), pltpu.VMEM((8, tq), F32),
                        pltpu.VMEM((NSA_GROUP, LANES, tq), F32), pltpu.VMEM((NSA_GROUP, LANES, tq), F32)],
        compiler_params=_cp("parallel", "parallel", "arbitrary"),
        name="nsa_attn",
    )(*qn, *qr, kc_bf, vct_bf, ks_bf, vst3, kw_bf, vwt3, gates)


def _tile_rows(x, reps):
    return jnp.concatenate([x] * reps, axis=0)


def _block_diag_q(q, n_heads):
    t = q.shape[0]
    assert n_heads * t <= LANES and q.shape[1] == n_heads * HEAD_DIM
    rows = _tile_rows(q, LANES // t)
    rh = jnp.right_shift(lax.broadcasted_iota(jnp.int32, rows.shape, 0), t.bit_length() - 1)
    lh = _head_of(lax.broadcasted_iota(jnp.int32, rows.shape, 1))
    return jnp.where(rh == lh, rows, 0.0).astype(BF)


def _gather_diag(o, n_heads, t):
    lh = _head_of(lax.broadcasted_iota(jnp.int32, (t, o.shape[1]), 1))
    out = jnp.zeros((t, o.shape[1]), F32)
    for h in range(n_heads):
        out = jnp.where(lh == h, o[h * t:(h + 1) * t, :], out)
    return out


def _pad_rows(pad_ref, x):
    pad_ref[...] = jnp.zeros_like(pad_ref)
    pad_ref[0:x.shape[0], :] = x
    return pad_ref[...].astype(BF)


def _sb_dec_kernel(pps, n_steps, pt_ref, q_ref, kn_ref, vn_ref, *rest):
    k_refs = rest[:pps]
    v_refs = rest[pps:2 * pps]
    u_ref, o_ref, qbd_ref, acc_ref, carry_ref, padk_ref, padv_ref = rest[2 * pps:]
    del pt_ref
    s = pl.program_id(1)
    t = q_ref.shape[0]
    u = u_ref[...]

    def block(kb, vb, new):
        z = _dot_nt(kb, qbd_ref[...])
        sp = jnp.maximum(z, 0.0) + jnp.log(1.0 + jnp.exp(-jnp.abs(z)))
        lk = -sp
        if new:
            kr = lax.broadcasted_iota(jnp.int32, z.shape, 0)
            tq = jnp.bitwise_and(lax.broadcasted_iota(jnp.int32, z.shape, 1), t - 1)
            mask = kr < tq
            lk = jnp.where(mask, lk, 0.0)
        hi, lo = _split_bf(lk)
        after = _dot(u, hi) + _dot(u, lo)
        c = carry_ref[0:1, :]
        a = jnp.exp((z - sp) + after + c)
        if new:
            a = jnp.where(mask, a, 0.0)
        carry_ref[0:1, :] = c + jnp.sum(lk, axis=0, keepdims=True)
        acc_ref[...] += _dot_tn(a.astype(BF), vb)

    @pl.when(s == 0)
    def _():
        qbd_ref[...] = _block_diag_q(q_ref[...], N_SELF_HEADS)
        acc_ref[...] = jnp.zeros_like(acc_ref)
        carry_ref[...] = jnp.zeros_like(carry_ref)
        block(_pad_rows(padk_ref, kn_ref[...]), _pad_rows(padv_ref, vn_ref[...]), True)

    for j in range(pps):
        block(k_refs[j][...].astype(BF), v_refs[j][...].astype(BF), False)

    @pl.when(s == n_steps - 1)
    def _():
        o_ref[...] = _gather_diag(acc_ref[...], N_SELF_HEADS, t)


def _sb_decode(pt, q, k_new, v_new, pool_k, pool_v, u_bf, *, layer, n_pages, pps):
    m = q.shape[0]
    t = m // (pt.shape[0] // n_pages)
    db = m // t
    n_steps = n_pages // pps
    row = pl.BlockSpec((t, SELF_WIDTH), lambda b, s, pt: (b, 0))

    def page_spec(j):
        return pl.BlockSpec((None, None, PAGE_SIZE, SELF_WIDTH),
                            lambda b, s, pt: (layer, pt[b * n_pages + n_pages - 1 - (s * pps + j)], 0, 0))

    pages = [page_spec(j) for j in range(pps)]
    return pl.pallas_call(
        functools.partial(_sb_dec_kernel, pps, n_steps),
        grid_spec=pltpu.PrefetchScalarGridSpec(
            num_scalar_prefetch=1,
            grid=(db, n_steps),
            in_specs=[row, row, row] + pages + pages + [pl.BlockSpec((LANES, LANES), lambda b, s, pt: (0, 0))],
            out_specs=row,
            scratch_shapes=[pltpu.VMEM((LANES, SELF_WIDTH), BF), pltpu.VMEM((LANES, SELF_WIDTH), F32),
                            pltpu.VMEM((8, LANES), F32), pltpu.VMEM((LANES, SELF_WIDTH), F32),
                            pltpu.VMEM((LANES, SELF_WIDTH), F32)]),
        out_shape=jax.ShapeDtypeStruct((m, SELF_WIDTH), F32),
        compiler_params=_cp("parallel", "arbitrary"),
        name="sb_decode",
    )(pt, q, k_new, v_new, *([pool_k] * pps), *([pool_v] * pps), u_bf)


def _mem_dec_kernel(q_ref, k_ref, v_ref, o_ref):
    t = q_ref.shape[0]
    qbd = _block_diag_q(q_ref[...], N_MEM_HEADS)
    s = _dot_nt(k_ref[...].astype(BF), qbd)
    e = jnp.exp(s - jnp.max(s, axis=0, keepdims=True))
    p = e / jnp.sum(e, axis=0, keepdims=True)
    o_ref[...] = _gather_diag(_dot_tn(p.astype(BF), v_ref[...].astype(BF)), N_MEM_HEADS, t)


def _mem_decode(qm, cache_k, cache_v, *, layer, t):
    m = qm.shape[0]
    row = pl.BlockSpec((t, MEM_WIDTH), lambda b: (b, 0))
    cache = pl.BlockSpec((None, None, N_MEM, MEM_WIDTH), lambda b: (layer, b, 0, 0))
    return pl.pallas_call(
        _mem_dec_kernel,
        grid=(m // t,),
        in_specs=[row, cache, cache],
        out_specs=row,
        out_shape=jax.ShapeDtypeStruct((m, MEM_WIDTH), F32),
        compiler_params=_cp("parallel"),
        name="mem_decode",
    )(qm, cache_k, cache_v)


def _cmp_dec_kernel(n_pages, pt_ref, *refs):
    kp = refs[:n_pages]
    vp = refs[n_pages:2 * n_pages]
    pek_ref, pev_ref, wk_ref, wv_ref, ok_ref, ov_ref, x_ref = refs[2 * n_pages:]
    del pt_ref
    rows = kp[0].shape[0]
    for pages, pe_ref, w_ref, o_ref in ((kp, pek_ref, wk_ref, ok_ref), (vp, pev_ref, wv_ref, ov_ref)):
        for p in range(n_pages):
            x_ref[p * rows:(p + 1) * rows, :] = pages[p][...]
        o_ref[...] = _compress_rows(x_ref, pe_ref, w_ref, o_ref.shape[0]).astype(o_ref.dtype)


def _cmp_decode(pt, pool_k, pool_v, pek2, pev2, wk2, wv2, *, layer, n_pages, db):
    n_cmp = n_pages * PAGE_SIZE // CMP_BLOCK

    def page_spec(p):
        return pl.BlockSpec((None, None, 2 * PAGE_SIZE, LANES), lambda b, pt: (layer, pt[b * n_pages + p], 0, 0))

    pages = [page_spec(p) for p in range(n_pages)]
    const2 = lambda b, pt: (0, 0)
    const3 = lambda b, pt: (0, 0, 0)
    out = pl.BlockSpec((None, n_cmp, KV_WIDTH), lambda b, pt: (b, 0, 0))
    return pl.pallas_call(
        functools.partial(_cmp_dec_kernel, n_pages),
        grid_spec=pltpu.PrefetchScalarGridSpec(
            num_scalar_prefetch=1,
            grid=(db,),
            in_specs=pages + pages + [pl.BlockSpec(pek2.shape, const2), pl.BlockSpec(pev2.shape, const2),
                                      pl.BlockSpec(wk2.shape, const3), pl.BlockSpec(wv2.shape, const3)],
            out_specs=[out, out],
            scratch_shapes=[pltpu.VMEM((n_pages * 2 * PAGE_SIZE, LANES), F32)]),
        out_shape=[jax.ShapeDtypeStruct((db, n_cmp, KV_WIDTH), BF)] * 2,
        compiler_params=_cp("parallel"),
        name="cmp_decode",
    )(pt, *([pool_k] * n_pages), *([pool_v] * n_pages), pek2, pev2, wk2, wv2)


def _nsa_dec_kernel(n_pages, past, pt_ref, qn0, qn1, qn2, qr0, qr1, qr2, gate_ref, kc_ref, vc_ref,
                    ksn_ref, vsn_ref, kwn_ref, vwn_ref, wk_ref, wv_ref, *rest):
    kp = rest[:n_pages]
    vp = rest[n_pages:2 * n_pages]
    o0, o1, o2, imp_ref, score_ref, sel_ref, ssel_ref, swin_ref, pada_ref, padb_ref = rest[2 * n_pages:]
    del pt_ref
    t = qn0.shape[0]
    t_bits = t.bit_length() - 1
    lane = lax.broadcasted_iota(jnp.int32, (1, LANES), 1)
    tq = jnp.bitwise_and(lane, t - 1)
    qpos = past + tq
    live = lane < N_SELF_HEADS * t

    def build(q_refs):
        parts = []
        for r in range(NSA_GROUP):
            rows = _tile_rows(q_refs[r][...], NSA_KV_HEADS)
            rg = jnp.right_shift(lax.broadcasted_iota(jnp.int32, rows.shape, 0), t_bits)
            lg = _head_of(lax.broadcasted_iota(jnp.int32, rows.shape, 1))
            parts.append(jnp.where(rg == lg, rows, 0.0))
        parts.append(jnp.zeros((LANES - N_SELF_HEADS * t, KV_WIDTH), F32))
        return jnp.concatenate(parts, axis=0).astype(BF)

    qbn = build((qn0, qn1, qn2))
    qbr = build((qr0, qr1, qr2))

    kc = kc_ref[...]
    n_cmp = kc.shape[0]
    nrow = lax.broadcasted_iota(jnp.int32, (n_cmp, LANES), 0)
    c_mask = (nrow * CMP_BLOCK + (CMP_BLOCK - 1)) <= qpos
    s = jnp.where(c_mask, _dot_nt(kc, qbn), NEG)
    e = jnp.exp(s - jnp.max(s, axis=0, keepdims=True))
    p = jnp.where(c_mask & live, e / jnp.sum(e, axis=0, keepdims=True), 0.0)
    o_cmp = _dot_tn(p.astype(BF), vc_ref[...])

    grp = NSA_KV_HEADS * t
    imp_ref[...] = p + pltpu.roll(p, grp, axis=1) + pltpu.roll(p, 2 * grp, axis=1) + pltpu.roll(p, 3 * grp, axis=1)
    ratio = SEL_BLOCK // CMP_BLOCK
    n_pair = n_cmp // ratio
    pair = imp_ref[pl.ds(0, n_pair, stride=ratio), :] + imp_ref[pl.ds(1, n_pair, stride=ratio), :]
    n_pad = score_ref.shape[0]
    blk_imp = jnp.concatenate([pair, jnp.zeros((n_pad - n_pair, LANES), F32)], axis=0)
    blk = lax.broadcasted_iota(jnp.int32, (n_pad, LANES), 0)
    cur = _head_of(qpos)
    forced = (blk == 0) | (blk == cur) | (blk == cur - 1)
    score_ref[...] = jnp.where(forced, jnp.inf, jnp.where(blk <= cur, blk_imp, -jnp.inf))
    sel_ref[...] = _topk_select(score_ref, n_pad, cur)

    def softmax_av(s_ref, n_blocks, v_of):
        e = jnp.exp(s_ref[...] - jnp.max(s_ref[...], axis=0, keepdims=True))
        s_ref[...] = e / jnp.sum(e, axis=0, keepdims=True)
        acc = jnp.zeros((LANES, KV_WIDTH), F32)
        for j in range(n_blocks):
            acc = acc + _dot_tn(s_ref[j * LANES:(j + 1) * LANES, :].astype(BF), v_of(j))
        return acc

    krow = lax.broadcasted_iota(jnp.int32, (LANES, LANES), 0)
    per_page = PAGE_SIZE // SEL_BLOCK

    for pg in range(n_pages):
        sc = _dot_nt(kp[pg][...].astype(BF), qbr)
        selv = jnp.concatenate(
            [jnp.broadcast_to(sel_ref[pg * per_page + u:pg * per_page + u + 1, :], (SEL_BLOCK, LANES))
             for u in range(per_page)], axis=0)
        ssel_ref[pg * LANES:(pg + 1) * LANES, :] = jnp.where(selv > 0.5, sc, NEG)
    new_blk = n_pages * per_page
    sc = _dot_nt(_pad_rows(pada_ref, ksn_ref[...]), qbr)
    ok = (jnp.broadcast_to(sel_ref[new_blk:new_blk + 1, :], (LANES, LANES)) > 0.5) & (krow <= tq) & (krow < t)
    ssel_ref[n_pages * LANES:(n_pages + 1) * LANES, :] = jnp.where(ok, sc, NEG)
    vs_new = _pad_rows(padb_ref, vsn_ref[...])
    o_sel = softmax_av(ssel_ref, n_pages + 1,
                       lambda j: vs_new if j == n_pages else vp[j][...].astype(BF))

    n_win = wk_ref.shape[0]
    w0 = past - n_win
    wrow = lax.broadcasted_iota(jnp.int32, (n_win, LANES), 0)
    wpos = w0 + wrow
    sc = _dot_nt(wk_ref[...].astype(BF), qbr)
    swin_ref[0:n_win, :] = jnp.where((wpos <= qpos) & (wpos > qpos - WINDOW), sc, NEG)
    sc = _dot_nt(_pad_rows(pada_ref, kwn_ref[...]), qbr)
    swin_ref[n_win:n_win + LANES, :] = jnp.where((krow <= tq) & (krow < t), sc, NEG)
    vw_new = _pad_rows(padb_ref, vwn_ref[...])
    n_wblk = n_win // LANES
    o_win = softmax_av(swin_ref, n_wblk + 1,
                       lambda j: vw_new if j == n_wblk else wv_ref[j * LANES:(j + 1) * LANES, :].astype(BF))

    gates = _tile_rows(_sigmoid(gate_ref[...]), LANES // t)
    row = lax.broadcasted_iota(jnp.int32, gates.shape, 0)
    col = lax.broadcasted_iota(jnp.int32, gates.shape, 1)
    r_of = jnp.right_shift(row, t_bits + 2)
    g_of = jnp.bitwise_and(jnp.right_shift(row, t_bits), NSA_KV_HEADS - 1)
    base = jnp.right_shift(g_of, 1) * LANES + r_of * 2 + jnp.bitwise_and(g_of, 1)

    def gate(branch):
        return jnp.sum(jnp.where(col == base + branch * 2 * NSA_GROUP, gates, 0.0), axis=1, keepdims=True)

    o = gate(0) * o_cmp + gate(1) * o_sel + gate(2) * o_win
    for r, o_ref in enumerate((o0, o1, o2)):
        o_ref[...] = _gather_diag(o[r * grp:(r + 1) * grp, :], NSA_KV_HEADS, t)


def _nsa_decode(pt, qn, qr, gates, kc, vc, new_rows, win_k, win_v, pool_k, pool_v, *, layer, n_pages, t):
    m = qn[0].shape[0]
    db = m // t
    past = n_pages * PAGE_SIZE
    n_cmp = kc.shape[1]
    n_win = win_k.shape[2]
    n_sel_pad = -(-(-(-(past + t) // SEL_BLOCK)) // 8) * 8
    row = pl.BlockSpec((t, KV_WIDTH), lambda b, pt: (b, 0))
    cmp_spec = pl.BlockSpec((None, n_cmp, KV_WIDTH), lambda b, pt: (b, 0, 0))
    win_spec = pl.BlockSpec((None, None, n_win, KV_WIDTH), lambda b, pt: (layer, b, 0, 0))

    def page_spec(p):
        return pl.BlockSpec((None, None, PAGE_SIZE, KV_WIDTH), lambda b, pt: (layer, pt[b * n_pages + p], 0, 0))

    pages = [page_spec(p) for p in range(n_pages)]
    return pl.pallas_call(
        functools.partial(_nsa_dec_kernel, n_pages, past),
        grid_spec=pltpu.PrefetchScalarGridSpec(
            num_scalar_prefetch=1,
            grid=(db,),
            in_specs=[row] * 7 + [cmp_spec, cmp_spec] + [row] * 4 + [win_spec, win_spec] + pages + pages,
            out_specs=[row] * 3,
            scratch_shapes=[pltpu.VMEM((n_cmp, LANES), F32), pltpu.VMEM((n_sel_pad, LANES), F32),
                            pltpu.VMEM((n_sel_pad, LANES), F32), pltpu.VMEM(((n_pages + 1) * LANES, LANES), F32),
                            pltpu.VMEM((n_win + LANES, LANES), F32), pltpu.VMEM((LANES, KV_WIDTH), F32),
                            pltpu.VMEM((LANES, KV_WIDTH), F32)]),
        out_shape=[jax.ShapeDtypeStruct((m, KV_WIDTH), F32)] * 3,
        compiler_params=_cp("parallel"),
        name="nsa_decode",
    )(pt, *qn, *qr, gates, kc, vc, *new_rows, win_k, win_v, *([pool_k] * n_pages), *([pool_v] * n_pages))


NSA_IN = SELF_WIDTH + 6 * KV_WIDTH + 3 * N_SELF_HEADS + MEM_WIDTH
ROW_TILE = 512
SB_PAGES_PER_STEP = 4


def _nsa_head_order():
    return [NSA_GROUP * g + r for r in range(NSA_GROUP) for g in range(NSA_KV_HEADS)]


def _nsa_in_columns():
    q = [h * HEAD_DIM + d for h in _nsa_head_order() for d in range(HEAD_DIM)]
    kv = list(range(SELF_WIDTH, SELF_WIDTH + 6 * KV_WIDTH))
    g0 = SELF_WIDTH + 6 * KV_WIDTH
    gate = []
    for gp in range(KV_WIDTH // LANES):
        for c in range(LANES):
            if c < 3 * 2 * NSA_GROUP:
                branch, r, hf = c // (2 * NSA_GROUP), (c % (2 * NSA_GROUP)) // 2, c % 2
                gate.append(g0 + (NSA_GROUP * (2 * gp + hf) + r) * 3 + branch)
            else:
                gate.append(NSA_IN)
    qm = list(range(g0 + 3 * N_SELF_HEADS, NSA_IN))
    cols = np.asarray(q + kv + gate + qm, np.int32)
    assert cols.shape[0] == _NSA_COLS
    return cols


def _nsa_out_rows():
    o = [h * HEAD_DIM + d for h in _nsa_head_order() for d in range(HEAD_DIM)]
    return np.asarray(o + list(range(SELF_WIDTH, SELF_WIDTH + MEM_WIDTH)), np.int32)


def _block_diag2(w):
    eye = jnp.eye(LANES // HEAD_DIM, dtype=w.dtype)
    return jnp.einsum("gh,lde->lgdhe", eye, w).reshape(w.shape[0], LANES, LANES).astype(BF)


def kernel(x_prompt, x_sample, mem_prompt, cache_sb_k, cache_sb_v, cache_nsa_cmp_k, cache_nsa_cmp_v, cache_nsa_sel_k, cache_nsa_sel_v, cache_nsa_win_k, cache_nsa_win_v, cache_mem_k, cache_mem_v, page_table, ln_mix_pre, ln_mix_post, ln_ffn_pre, ln_ffn_post, ln_mem, w_in_a, w_in_b, w_cmp_k, w_cmp_v, pe_cmp_k, pe_cmp_v, w_out, w_mem_kv, w_gate_up, w_down):
    bsz, seq, d = x_prompt.shape
    db, ds, _ = x_sample.shape
    n_pages = page_table.shape[1]
    past = n_pages * PAGE_SIZE
    depth = w_out.shape[0]
    n_pool = cache_sb_k.shape[1]
    n_win = cache_nsa_win_k.shape[2]
    tm_p = min(ROW_TILE, bsz * seq)
    tm_s = min(ROW_TILE, db * ds)
    assert d == D_MODEL and seq % KEY_BLOCK == 0 and seq % tm_p == 0 and (db * ds) % tm_s == 0 and tm_s % ds == 0
    assert ds & (ds - 1) == 0 and N_SELF_HEADS * ds <= LANES and ds < CMP_BLOCK
    assert past % SEL_BLOCK == 0 and past + ds <= past + SEL_BLOCK and n_win % LANES == 0 and n_win <= past
    assert n_pages % SB_PAGES_PER_STEP == 0 and seq >= WINDOW

    h_p = x_prompt.reshape(bsz * seq, d)
    h_s = x_sample.reshape(db * ds, d)
    mem = mem_prompt.reshape(bsz * N_MEM, d)
    pt = page_table.reshape(-1).astype(jnp.int32)
    u256 = jnp.triu(jnp.ones((KEY_BLOCK, KEY_BLOCK), BF), 1)
    u128 = jnp.triu(jnp.ones((LANES, LANES), BF), 1)
    rope_p = _rope_tables(jnp.arange(seq, dtype=jnp.int32))
    rope_s = _rope_tables(past + jnp.arange(tm_s, dtype=jnp.int32) % ds)

    sbk_pool = cache_sb_k.reshape(-1, n_pool, PAGE_SIZE, SELF_WIDTH)
    sbv_pool = cache_sb_v.reshape(-1, n_pool, PAGE_SIZE, SELF_WIDTH)
    cmpk_pool = cache_nsa_cmp_k.reshape(-1, n_pool, 2 * PAGE_SIZE, LANES)
    cmpv_pool = cache_nsa_cmp_v.reshape(-1, n_pool, 2 * PAGE_SIZE, LANES)
    selk_pool = cache_nsa_sel_k.reshape(-1, n_pool, PAGE_SIZE, KV_WIDTH)
    selv_pool = cache_nsa_sel_v.reshape(-1, n_pool, PAGE_SIZE, KV_WIDTH)
    wink = cache_nsa_win_k.reshape(-1, db, n_win, KV_WIDTH)
    winv = cache_nsa_win_v.reshape(-1, db, n_win, KV_WIDTH)
    memk = cache_mem_k.reshape(depth, db, N_MEM, MEM_WIDTH)
    memv = cache_mem_v.reshape(depth, db, N_MEM, MEM_WIDTH)
    nsa_cols = _nsa_in_columns()
    nsa_rows = _nsa_out_rows()

    sb_p, sb_s = [[], []], [[], []]
    nsa_p, nsa_s = [[] for _ in range(6)], [[] for _ in range(6)]
    memk_p, memv_p = [], []
    for i in range(depth):
        j = i // 2
        g_pre = ln_mix_pre[i][None]
        tail_w = (ln_mix_post[i][None], ln_ffn_pre[i][None], ln_ffn_post[i][None],
                  w_gate_up[i].astype(BF), w_down[i].astype(BF))
        wkv = w_mem_kv[i]
        mk, mv, mk_bf, mvt_bf = _memkv(mem, ln_mem[i][None], wkv.astype(BF), wkv[:, MEM_WIDTH:].T.astype(BF))
        memk_p.append(mk.reshape(bsz, N_MEM, N_MEM_HEADS, HEAD_DIM))
        memv_p.append(mv.reshape(bsz, N_MEM, N_MEM_HEADS, HEAD_DIM))
        if i % 2 == 0:
            w = w_in_a[j]
            w_bf = w.astype(BF)
            wvt = w[:, 2 * SELF_WIDTH:3 * SELF_WIDTH].T.astype(BF)
            q, k, v, kb, qm, vt3 = _sb_inproj(h_p, g_pre, w_bf, wvt, tm=tm_p, with_vt=True, q_dtype=BF)
            o = _sb_attn(q, kb, vt3, u256, batch=bsz, seq=seq)
            om = _mem_attn(qm, mk_bf, mvt_bf, batch=bsz, seq=seq)
            qs, ks, vs, _, qms = _sb_inproj(h_s, g_pre, w_bf, wvt, tm=tm_s, with_vt=False, q_dtype=F32)
            o_s = _sb_decode(pt, qs, ks, vs, sbk_pool, sbv_pool, u128, layer=j, n_pages=n_pages,
                             pps=SB_PAGES_PER_STEP)
            om_s = _mem_decode(qms, memk, memv, layer=i, t=ds)
            wo = w_out[i].astype(BF)
            mix_p, mix_s = [o, om], [o_s, om_s]
            for lst, a, shape in ((sb_p, (k, v), (bsz, seq)), (sb_s, (ks, vs), (db, ds))):
                for n in range(2):
                    lst[n].append(a[n].reshape(*shape, N_SELF_HEADS, HEAD_DIM))
        else:
            w = w_in_b[j]
            w_bf = jnp.concatenate([w, jnp.zeros((d, 1), w.dtype)], axis=1)[:, nsa_cols].astype(BF)
            vs0 = SELF_WIDTH + 3 * KV_WIDTH
            vw0 = SELF_WIDTH + 5 * KV_WIDTH
            wt = jnp.concatenate([w[:, vs0:vs0 + KV_WIDTH], w[:, vw0:vw0 + KV_WIDTH]], axis=1).T.astype(BF)
            pek2 = jnp.tile(pe_cmp_k[j], (1, LANES // HEAD_DIM))
            pev2 = jnp.tile(pe_cmp_v[j], (1, LANES // HEAD_DIM))
            wck2 = _block_diag2(w_cmp_k[j])
            wcv2 = _block_diag2(w_cmp_v[j])
            outs = _nsa_inproj(h_p, g_pre, w_bf, wt, *rope_p, tm=tm_p, with_vt=True, q_dtype=BF)
            qn, qr, rows_p = outs[0:3], outs[3:6], outs[6:12]
            ksb, kwb, gates, qm, vst3, vwt3 = outs[12:18]
            kc_bf, _ = _compress(rows_p[0].reshape(-1, LANES), pek2, wck2, rows=seq)
            _, vct_bf = _compress(rows_p[1].reshape(-1, LANES), pev2, wcv2, rows=seq)
            o3 = _nsa_attn(qn, qr, kc_bf, vct_bf, ksb, vst3, kwb, vwt3, gates, batch=bsz, seq=seq)
            om = _mem_attn(qm, mk_bf, mvt_bf, batch=bsz, seq=seq)
            outs = _nsa_inproj(h_s, g_pre, w_bf, wt, *rope_s, tm=tm_s, with_vt=False, q_dtype=F32)
            qn_s, qr_s, rows_s = outs[0:3], outs[3:6], outs[6:12]
            gates_s, qms = outs[14], outs[15]
            kc_s, vc_s = _cmp_decode(pt, cmpk_pool, cmpv_pool, pek2, pev2, wck2, wcv2, layer=j, n_pages=n_pages,
                                     db=db)
            o3_s = _nsa_decode(pt, qn_s, qr_s, gates_s, kc_s, vc_s, rows_s[2:6], wink, winv, selk_pool, selv_pool,
                               layer=j, n_pages=n_pages, t=ds)
            om_s = _mem_decode(qms, memk, memv, layer=i, t=ds)
            wo = w_out[i][nsa_rows].astype(BF)
            mix_p, mix_s = [*o3, om], [*o3_s, om_s]
            for n in range(6):
                rp = rows_p[n].reshape(bsz, seq, NSA_KV_HEADS, HEAD_DIM)
                nsa_p[n].append(rp[:, seq - min(WINDOW, seq):] if n >= 4 else rp)
                nsa_s[n].append(rows_s[n].reshape(db, ds, NSA_KV_HEADS, HEAD_DIM))
        h_p = _layer_tail(h_p, mix_p, wo, *tail_w, tm=tm_p)
        h_s = _layer_tail(h_s, mix_s, wo, *tail_w, tm=tm_s)

    st = lambda rows: jnp.stack(rows, axis=0)
    return (h_p.reshape(bsz, seq, d), h_s.reshape(db, ds, d),
            st(sb_p[0]), st(sb_p[1]),
            *[st(x) for x in nsa_p],
            st(memk_p), st(memv_p),
            st(sb_s[0]), st(sb_s[1]),
            *[st(x) for x in nsa_s])
```

```python
import functools

import jax
import jax.numpy as jnp
import numpy as np
from jax import lax
from jax.experimental import pallas as pl
from jax.experimental.pallas import tpu as pltpu

F32 = jnp.float32
BF = jnp.bfloat16

D_MODEL = 1024
HEAD_DIM = 64
N_SELF_HEADS = 12
N_MEM_HEADS = 4
SELF_WIDTH = N_SELF_HEADS * HEAD_DIM
MEM_WIDTH = N_MEM_HEADS * HEAD_DIM
N_MEM = 256
NSA_KV_HEADS = 4
NSA_GROUP = N_SELF_HEADS // NSA_KV_HEADS
KV_WIDTH = NSA_KV_HEADS * HEAD_DIM
CMP_BLOCK = 32
SEL_BLOCK = 64
SEL_TOPK = 16
WINDOW = 512
PAGE_SIZE = 128
ROPE_THETA = 500000.0
ROT_DIM = HEAD_DIM // 4
EPS = 1e-6
NEG = -1e30
SCALE = HEAD_DIM ** -0.5
Q_SCALE = SCALE * 1.4426950408889634

LANES = 128
KEY_BLOCK = 256
VMEM_LIMIT = 56 * 1024 * 1024


def _cp(*sem):
    return pltpu.CompilerParams(dimension_semantics=sem, vmem_limit_bytes=VMEM_LIMIT)


def _dot(a, b):
    return jnp.dot(a, b, preferred_element_type=F32)


def _dot_nt(a, b):
    return lax.dot_general(a, b, (((1,), (1,)), ((), ())), preferred_element_type=F32)


def _dot_tn(a, b):
    return lax.dot_general(a, b, (((0,), (0,)), ((), ())), preferred_element_type=F32)


def _rms(x, g):
    return x * lax.rsqrt(jnp.mean(x * x, axis=-1, keepdims=True) + EPS) * g


def _sigmoid(x):
    return 1.0 / (1.0 + jnp.exp(-x))


def _neg_abs(y):
    bits = lax.bitcast_convert_type(y, jnp.int32) | jnp.int32(-2 ** 31)
    return lax.bitcast_convert_type(bits, F32)


def _sb_log_keep(y):
    return jnp.minimum(y, 0.0) - jnp.log2(1.0 + jnp.exp2(_neg_abs(y)))


def _head_of(idx):
    return jnp.right_shift(idx, 6)


def _const_spec(shape):
    nd = len(shape)
    return pl.BlockSpec(shape, lambda *_: (0,) * nd, pipeline_mode=pl.Buffered(1))


def _rope128(x, c, s):
    lane = lax.broadcasted_iota(jnp.int32, x.shape, 1)
    up = pltpu.roll(x, LANES - ROT_DIM // 2, axis=1)
    dn = pltpu.roll(x, ROT_DIM // 2, axis=1)
    sw = jnp.where(jnp.bitwise_and(lane, HEAD_DIM - 1) < ROT_DIM // 2, up, dn)
    return x * c + sw * s


def _rope_tables(pos):
    half = ROT_DIM // 2
    inv = ROPE_THETA ** (-jnp.arange(half, dtype=F32) * 2.0 / ROT_DIM)
    ang = pos.astype(F32)[:, None] * inv[None, :]
    cos, sin = jnp.cos(ang), jnp.sin(ang)
    n = pos.shape[0]
    c64 = jnp.concatenate([cos, cos, jnp.ones((n, HEAD_DIM - ROT_DIM), F32)], axis=1)
    s64 = jnp.concatenate([-sin, sin, jnp.zeros((n, HEAD_DIM - ROT_DIM), F32)], axis=1)
    return jnp.tile(c64, (1, LANES // HEAD_DIM)), jnp.tile(s64, (1, LANES // HEAD_DIM))


def _sb_inproj_kernel(x_ref, g_ref, w_ref, wvt_ref, q_ref, k_ref, v_ref, kb_ref, qm_ref, *vt_refs):
    xn = _rms(x_ref[...], g_ref[...]).astype(BF)
    q_ref[...] = (_dot(xn, w_ref[:, 0:SELF_WIDTH]) * -Q_SCALE).astype(q_ref.dtype)
    k = _dot(xn, w_ref[:, SELF_WIDTH:2 * SELF_WIDTH])
    k_ref[...] = k
    kb_ref[...] = k.astype(BF)
    v_ref[...] = _dot(xn, w_ref[:, 2 * SELF_WIDTH:3 * SELF_WIDTH])
    qm_ref[...] = (_dot(xn, w_ref[:, 3 * SELF_WIDTH:3 * SELF_WIDTH + MEM_WIDTH]) * Q_SCALE).astype(qm_ref.dtype)
    if vt_refs:
        (vt_ref,) = vt_refs
        vt = _dot_nt(wvt_ref[...], xn)
        for c in range(vt_ref.shape[0]):
            vt_ref[c] = vt[:, c * KEY_BLOCK:(c + 1) * KEY_BLOCK].astype(BF)


def _sb_inproj(x, g, w_bf, wvt_bf, *, tm, with_vt, q_dtype):
    m = x.shape[0]
    grid = (m // tm,)
    row = lambda i: (i, 0)
    const = lambda i: (0, 0)
    out_shape = [jax.ShapeDtypeStruct((m, SELF_WIDTH), q_dtype),
                 jax.ShapeDtypeStruct((m, SELF_WIDTH), F32),
                 jax.ShapeDtypeStruct((m, SELF_WIDTH), F32),
                 jax.ShapeDtypeStruct((m, SELF_WIDTH), BF),
                 jax.ShapeDtypeStruct((m, MEM_WIDTH), q_dtype)]
    out_specs = [pl.BlockSpec((tm, SELF_WIDTH), row)] * 4 + [pl.BlockSpec((tm, MEM_WIDTH), row)]
    if with_vt:
        out_shape.append(jax.ShapeDtypeStruct((m // KEY_BLOCK, SELF_WIDTH, KEY_BLOCK), BF))
        out_specs.append(pl.BlockSpec((tm // KEY_BLOCK, SELF_WIDTH, KEY_BLOCK), lambda i: (i, 0, 0)))
    return pl.pallas_call(
        _sb_inproj_kernel,
        grid=grid,
        in_specs=[pl.BlockSpec((tm, D_MODEL), row), _const_spec((1, D_MODEL)),
                  _const_spec(w_bf.shape), _const_spec(wvt_bf.shape)],
        out_specs=out_specs,
        out_shape=out_shape,
        compiler_params=_cp("parallel"),
        name="sb_inproj",
    )(x, g, w_bf, wvt_bf)


_NSA_Q0 = 0
_NSA_KV0 = SELF_WIDTH
_NSA_G0 = _NSA_KV0 + 6 * KV_WIDTH
_NSA_QM0 = _NSA_G0 + 2 * LANES
_NSA_COLS = _NSA_QM0 + MEM_WIDTH


def _nsa_inproj_kernel(x_ref, g_ref, w_ref, wt_ref, c_ref, s_ref,
                       qn0, qn1, qn2, qr0, qr1, qr2, kc_ref, vc_ref, ks_ref, vs_ref, kw_ref, vw_ref,
                       ksb_ref, kwb_ref, gate_ref, qm_ref, *vt_refs):
    xn = _rms(x_ref[...], g_ref[...]).astype(BF)
    c = c_ref[...]
    s = s_ref[...]

    def rope256(y):
        return jnp.concatenate([_rope128(y[:, :LANES], c, s), _rope128(y[:, LANES:], c, s)], axis=1)

    for r, (qn_ref, qr_ref) in enumerate(((qn0, qr0), (qn1, qr1), (qn2, qr2))):
        q = _dot(xn, w_ref[:, r * KV_WIDTH:(r + 1) * KV_WIDTH]) * Q_SCALE
        qn_ref[...] = q.astype(qn_ref.dtype)
        qr_ref[...] = rope256(q).astype(qr_ref.dtype)

    def kv(j):
        return _dot(xn, w_ref[:, _NSA_KV0 + j * KV_WIDTH:_NSA_KV0 + (j + 1) * KV_WIDTH])

    kc_ref[...] = kv(0)
    vc_ref[...] = kv(1)
    ks = rope256(kv(2))
    ks_ref[...] = ks
    ksb_ref[...] = ks.astype(BF)
    vs_ref[...] = kv(3)
    kw = rope256(kv(4))
    kw_ref[...] = kw
    kwb_ref[...] = kw.astype(BF)
    vw_ref[...] = kv(5)
    gate_ref[...] = _dot(xn, w_ref[:, _NSA_G0:_NSA_G0 + 2 * LANES])
    qm_ref[...] = (_dot(xn, w_ref[:, _NSA_QM0:_NSA_QM0 + MEM_WIDTH]) * Q_SCALE).astype(qm_ref.dtype)
    if vt_refs:
        vst_ref, vwt_ref = vt_refs
        vt = _dot_nt(wt_ref[...], xn)
        for cblk in range(vst_ref.shape[0]):
            sl = slice(cblk * KEY_BLOCK, (cblk + 1) * KEY_BLOCK)
            vst_ref[cblk] = vt[:KV_WIDTH, sl].astype(BF)
            vwt_ref[cblk] = vt[KV_WIDTH:, sl].astype(BF)


def _nsa_inproj(x, g, w_bf, wt_bf, ctab, stab, *, tm, with_vt, q_dtype):
    m = x.shape[0]
    grid = (m // tm,)
    row = lambda i: (i, 0)
    const = lambda i: (0, 0)
    tbl_blocks = ctab.shape[0] // tm
    tbl = lambda i: (i % tbl_blocks, 0)
    kvs = lambda dt: jax.ShapeDtypeStruct((m, KV_WIDTH), dt)
    out_shape = [kvs(q_dtype)] * 6 + [kvs(F32)] * 6 + [kvs(BF)] * 2 + [kvs(F32), kvs(q_dtype)]
    out_specs = [pl.BlockSpec((tm, KV_WIDTH), row)] * 16
    if with_vt:
        out_shape += [jax.ShapeDtypeStruct((m // KEY_BLOCK, KV_WIDTH, KEY_BLOCK), BF)] * 2
        out_specs += [pl.BlockSpec((tm // KEY_BLOCK, KV_WIDTH, KEY_BLOCK), lambda i: (i, 0, 0))] * 2
    return pl.pallas_call(
        _nsa_inproj_kernel,
        grid=grid,
        in_specs=[pl.BlockSpec((tm, D_MODEL), row), _const_spec((1, D_MODEL)),
                  _const_spec(w_bf.shape), _const_spec(wt_bf.shape),
                  pl.BlockSpec((tm, LANES), tbl), pl.BlockSpec((tm, LANES), tbl)],
        out_specs=out_specs,
        out_shape=out_shape,
        compiler_params=_cp("parallel"),
        name="nsa_inproj",
    )(x, g, w_bf, wt_bf, ctab, stab)


def _memkv_kernel(x_ref, g_ref, w_ref, wvt_ref, k_ref, v_ref, kb_ref, vt_ref):
    xn = _rms(x_ref[...], g_ref[...]).astype(BF)
    k = _dot(xn, w_ref[:, :MEM_WIDTH])
    k_ref[...] = k
    kb_ref[...] = k.astype(BF)
    v_ref[...] = _dot(xn, w_ref[:, MEM_WIDTH:])
    vt_ref[...] = _dot_nt(wvt_ref[...], xn).astype(BF)


def _memkv(x, g, w_bf, wvt_bf):
    m = x.shape[0]
    row = lambda i: (i, 0)
    const = lambda i: (0, 0)
    return pl.pallas_call(
        _memkv_kernel,
        grid=(m // N_MEM,),
        in_specs=[pl.BlockSpec((N_MEM, D_MODEL), row), _const_spec((1, D_MODEL)),
                  _const_spec(w_bf.shape), _const_spec(wvt_bf.shape)],
        out_specs=[pl.BlockSpec((N_MEM, MEM_WIDTH), row)] * 3 + [pl.BlockSpec((MEM_WIDTH, N_MEM), lambda i: (0, i))],
        out_shape=[jax.ShapeDtypeStruct((m, MEM_WIDTH), F32)] * 2 + [jax.ShapeDtypeStruct((m, MEM_WIDTH), BF),
                                                                     jax.ShapeDtypeStruct((MEM_WIDTH, m), BF)],
        compiler_params=_cp("parallel"),
        name="memkv",
    )(x, g, w_bf, wvt_bf)


def _tail_kernel(n_mix, *refs):
    h_ref = refs[0]
    mix_refs = refs[1:1 + n_mix]
    wo_ref, gpost_ref, gpre_ref, gfpost_ref, wgu_ref, wd_ref, o_ref = refs[1 + n_mix:]
    y = None
    off = 0
    for mref in mix_refs:
        w = mref.shape[1]
        part = _dot(mref[...].astype(BF), wo_ref[off:off + w, :])
        y = part if y is None else y + part
        off += w
    h = h_ref[...] + _rms(y, gpost_ref[...])
    xn = _rms(h, gpre_ref[...]).astype(BF)
    d_ff = wd_ref.shape[0]
    acc = None
    for c0 in range(0, d_ff, KEY_BLOCK):
        gt = _dot(xn, wgu_ref[:, c0:c0 + KEY_BLOCK])
        up = _dot(xn, wgu_ref[:, d_ff + c0:d_ff + c0 + KEY_BLOCK])
        act = (gt * _sigmoid(gt) * up).astype(BF)
        part = _dot(act, wd_ref[c0:c0 + KEY_BLOCK, :])
        acc = part if acc is None else acc + part
    o_ref[...] = h + _rms(acc, gfpost_ref[...])


def _layer_tail(h, mixes, wo_bf, g_post, g_pre, g_fpost, wgu_bf, wd_bf, *, tm):
    m = h.shape[0]
    row = lambda i: (i, 0)
    const = lambda i: (0, 0)
    return pl.pallas_call(
        functools.partial(_tail_kernel, len(mixes)),
        grid=(m // tm,),
        in_specs=[pl.BlockSpec((tm, D_MODEL), row)] + [pl.BlockSpec((tm, x.shape[1]), row) for x in mixes]
        + [_const_spec(wo_bf.shape)] + [_const_spec((1, D_MODEL))] * 3
        + [_const_spec(wgu_bf.shape), _const_spec(wd_bf.shape)],
        out_specs=pl.BlockSpec((tm, D_MODEL), row),
        out_shape=jax.ShapeDtypeStruct((m, D_MODEL), F32),
        compiler_params=_cp("parallel"),
        name="layer_tail",
    )(h, *mixes, wo_bf, g_post, g_pre, g_fpost, wgu_bf, wd_bf)


def _half_masks(shape, axis):
    idx = lax.broadcasted_iota(jnp.int32, shape, axis)
    return idx < HEAD_DIM, idx >= HEAD_DIM


def _sb_attn_kernel(q_ref, k_ref, vt_ref, u_ref, o_ref, acc_ref, carry_ref):
    tq = q_ref.shape[0]
    i = pl.program_id(2)
    qi = q_ref[...]
    lo_l, hi_l = _half_masks(qi.shape, 1)
    zero = jnp.zeros_like(qi)
    qh = (jnp.where(lo_l, qi, zero), jnp.where(hi_l, qi, zero))
    acc_ref[...] = jnp.zeros_like(acc_ref)
    carry_ref[...] = jnp.zeros_like(carry_ref)
    u = u_ref[...]

    def run(blocks):
        carry = [carry_ref[h:h + 1, :] for h in range(2)]
        staged = []
        for j, diag in blocks:
            kb = k_ref[pl.ds(pl.multiple_of(j * tq, tq), tq), :]
            vt = vt_ref[j]
            lo_r, hi_r = _half_masks(vt.shape, 0)
            zv = jnp.zeros_like(vt)
            vth = (jnp.where(lo_r, vt, zv), jnp.where(hi_r, vt, zv))
            mask = None
            if diag:
                kr = lax.broadcasted_iota(jnp.int32, (tq, tq), 0)
                qc = lax.broadcasted_iota(jnp.int32, (tq, tq), 1)
                mask = kr < qc
            for h in range(2):
                y = _dot_nt(kb, qh[h])
                lk = _sb_log_keep(y)
                if diag:
                    lk = jnp.where(mask, lk, 0.0)
                after = _dot(u, lk.astype(BF))
                staged.append((h, y, lk, after, vth[h], mask))
        contrib = None
        for h, y, lk, after, vth_h, mask in staged:
            a = jnp.exp2((lk - y) + after + carry[h])
            if mask is not None:
                a = jnp.where(mask, a, 0.0)
            carry[h] = carry[h] + after[0:1, :] + lk[0:1, :]
            part = _dot(vth_h, a.astype(BF))
            contrib = part if contrib is None else contrib + part
        acc_ref[...] += contrib
        for h in range(2):
            carry_ref[h:h + 1, :] = carry[h]

    run([(i, True)])
    odd = jnp.bitwise_and(i, 1)

    @pl.when(odd == 1)
    def _():
        run([(i - 1, False)])

    def body(jj, c):
        j = i - 1 - odd - 2 * jj
        run([(j, False), (j - 1, False)])
        return c

    lax.fori_loop(0, jnp.right_shift(i, 1), body, 0)
    o_ref[...] = acc_ref[...].T.astype(o_ref.dtype)


def _sb_attn(q_bf, k_bf, vt3, u_bf, *, batch, seq):
    m = q_bf.shape[0]
    tq = KEY_BLOCK
    nq = seq // tq
    return pl.pallas_call(
        _sb_attn_kernel,
        grid=(batch, SELF_WIDTH // LANES, nq),
        in_specs=[pl.BlockSpec((tq, LANES), lambda b, p, i: (b * nq + i, p)),
                  pl.BlockSpec((seq, LANES), lambda b, p, i: (b, p)),
                  pl.BlockSpec((nq, LANES, tq), lambda b, p, i: (b, p, 0)),
                  _const_spec((tq, tq))],
        out_specs=pl.BlockSpec((tq, LANES), lambda b, p, i: (b * nq + i, p)),
        out_shape=jax.ShapeDtypeStruct((m, SELF_WIDTH), BF),
        scratch_shapes=[pltpu.VMEM((LANES, tq), F32), pltpu.VMEM((8, tq), F32)],
        compiler_params=_cp("parallel", "parallel", "arbitrary"),
        name="sb_attn",
    )(q_bf, k_bf, vt3, u_bf)


def _mem_attn_kernel(q_ref, k_ref, vt_ref, o_ref):
    qi = q_ref[...]
    k = k_ref[...]
    vt = vt_ref[...]
    lane_h = _head_of(lax.broadcasted_iota(jnp.int32, qi.shape, 1))
    row_h = _head_of(lax.broadcasted_iota(jnp.int32, vt.shape, 0))
    acc = None
    for h in range(N_MEM_HEADS):
        s = _dot_nt(k, jnp.where(lane_h == h, qi, jnp.zeros_like(qi)))
        e = jnp.exp2(s - jnp.max(s, axis=0, keepdims=True))
        p = e / jnp.sum(e, axis=0, keepdims=True)
        part = _dot(jnp.where(row_h == h, vt, jnp.zeros_like(vt)), p.astype(BF))
        acc = part if acc is None else acc + part
    o_ref[...] = acc.T.astype(o_ref.dtype)


def _mem_attn(qm_bf, mk_bf, mvt_bf, *, batch, seq):
    m = qm_bf.shape[0]
    tq = KEY_BLOCK
    nq = seq // tq
    return pl.pallas_call(
        _mem_attn_kernel,
        grid=(batch, nq),
        in_specs=[pl.BlockSpec((tq, MEM_WIDTH), lambda b, i: (b * nq + i, 0)),
                  pl.BlockSpec((N_MEM, MEM_WIDTH), lambda b, i: (b, 0)),
                  pl.BlockSpec((MEM_WIDTH, N_MEM), lambda b, i: (0, b))],
        out_specs=pl.BlockSpec((tq, MEM_WIDTH), lambda b, i: (b * nq + i, 0)),
        out_shape=jax.ShapeDtypeStruct((m, MEM_WIDTH), BF),
        compiler_params=_cp("parallel", "parallel"),
        name="mem_attn",
    )(qm_bf, mk_bf, mvt_bf)


def _compress_rows(xa_ref, xb_ref, pe_ref, w_ref, nblk):
    acc = None
    for l in range(CMP_BLOCK):
        xs = [ref[pl.ds(l, nblk, stride=CMP_BLOCK), :] for ref in (xa_ref, xb_ref)]
        xl = jnp.concatenate(xs, axis=0) + pe_ref[l:l + 1, :]
        part = _dot(xl.astype(BF), w_ref[l])
        acc = part if acc is None else acc + part
    return jnp.concatenate([acc[:nblk], acc[nblk:]], axis=1)


def _compress_kernel(xa_ref, xb_ref, pe_ref, w_ref, o_ref, ot_ref):
    out = _compress_rows(xa_ref, xb_ref, pe_ref, w_ref, o_ref.shape[0])
    o_ref[...] = out.astype(o_ref.dtype)
    ot_ref[...] = out.T.astype(ot_ref.dtype)


def _compress(x, pe2, wbd2, *, rows):
    m = x.shape[0]
    nblk = rows // CMP_BLOCK
    return pl.pallas_call(
        _compress_kernel,
        grid=(m // rows,),
        in_specs=[pl.BlockSpec((rows, LANES), lambda i: (i, 0)), pl.BlockSpec((rows, LANES), lambda i: (i, 1)),
                  _const_spec(pe2.shape), _const_spec(wbd2.shape)],
        out_specs=[pl.BlockSpec((nblk, KV_WIDTH), lambda i: (i, 0)), pl.BlockSpec((KV_WIDTH, nblk), lambda i: (0, i))],
        out_shape=[jax.ShapeDtypeStruct((m // CMP_BLOCK, KV_WIDTH), BF),
                   jax.ShapeDtypeStruct((KV_WIDTH, m // CMP_BLOCK), BF)],
        compiler_params=_cp("parallel"),
        name="compress",
    )(x, x, pe2, wbd2)


def _topk_select(score_ref, n_blocks, cur):
    score = score_ref[...]
    blk = lax.broadcasted_iota(jnp.int32, score.shape, 0)

    def body(i, cnt):
        si = score_ref[pl.ds(i, 1), :]
        tie = jnp.where(blk > i, 1.0, 0.0)
        return cnt + jnp.where(si > score, 1.0, jnp.where(si == score, tie, 0.0))

    rank = lax.fori_loop(0, n_blocks, body, jnp.zeros(score.shape, F32))
    return jnp.where((rank < SEL_TOPK) & (blk <= cur), 1.0, 0.0).astype(F32)


def _nsa_attn_kernel(qn0, qn1, qn2, qr0, qr1, qr2, kc_ref, vct_ref, ks_ref, vst_ref, kw_ref, vwt_ref, gate_ref,
                     o0, o1, o2, imp_ref, score_ref, sel_ref, m_ref, l_ref, acc_ref, occ_ref):
    tq = qn0.shape[0]
    tk = tq
    i = pl.program_id(2)
    qn_refs = (qn0, qn1, qn2)
    qr_refs = (qr0, qr1, qr2)
    o_refs = (o0, o1, o2)
    qpos = i * tq + lax.broadcasted_iota(jnp.int32, (1, tq), 1)

    def halves(x):
        lo, hi = _half_masks(x.shape, 1)
        z = jnp.zeros_like(x)
        return jnp.where(lo, x, z), jnp.where(hi, x, z)

    def by_row_half(a, b):
        lo, _ = _half_masks((LANES, tq), 0)
        return jnp.where(lo, a, b)

    def vhalves(vt):
        lo, hi = _half_masks(vt.shape, 0)
        z = jnp.zeros_like(vt)
        return jnp.where(lo, vt, z), jnp.where(hi, vt, z)

    n_cmp = kc_ref.shape[0]
    kc = kc_ref[...]
    vct = vhalves(vct_ref[...])
    nrow = lax.broadcasted_iota(jnp.int32, (n_cmp, tq), 0)
    c_mask = (nrow * CMP_BLOCK + (CMP_BLOCK - 1)) <= qpos
    imp = [None, None]
    for r in range(NSA_GROUP):
        qh = halves(qn_refs[r][...])
        occ = None
        for hf in range(2):
            s = jnp.where(c_mask, _dot_nt(kc, qh[hf]), NEG)
            e = jnp.exp2(s - jnp.max(s, axis=0, keepdims=True))
            p = jnp.where(c_mask, e / jnp.sum(e, axis=0, keepdims=True), 0.0)
            imp[hf] = p if imp[hf] is None else imp[hf] + p
            part = _dot(vct[hf], p.astype(BF))
            occ = part if occ is None else occ + part
        occ_ref[r] = occ

    n_sel = sel_ref.shape[1]
    ratio = SEL_BLOCK // CMP_BLOCK
    blk = lax.broadcasted_iota(jnp.int32, (n_sel, tq), 0)
    cur = _head_of(qpos)
    forced = (blk == 0) | (blk == cur) | (blk == cur - 1)
    allowed = blk <= cur
    scores = []
    for hf in range(2):
        parts = []
        for c in range(tq // LANES):
            imp_ref[c] = imp[hf][:, c * LANES:(c + 1) * LANES]
            parts.append(imp_ref[c, pl.ds(0, n_sel, stride=ratio), :] + imp_ref[c, pl.ds(1, n_sel, stride=ratio), :])
        blk_imp = jnp.concatenate(parts, axis=1)
        scores.append(jnp.where(forced, jnp.inf, jnp.where(allowed, blk_imp, -jnp.inf)))
    for hf in range(2):
        score_ref[...] = scores[hf]
        sel_ref[hf] = _topk_select(score_ref, n_sel, cur)

    def attend(q_halves, kb, vt, masks):
        vth = vhalves(vt)
        for r in range(NSA_GROUP):
            contrib = None
            alphas = []
            for hf in range(2):
                hd = 2 * r + hf
                s = jnp.where(masks[hf], _dot_nt(kb, q_halves[r][hf]), NEG)
                m_old = m_ref[hd:hd + 1, :]
                m_new = jnp.maximum(m_old, jnp.max(s, axis=0, keepdims=True))
                alpha = jnp.exp2(m_old - m_new)
                p = jnp.exp2(s - m_new)
                l_ref[hd:hd + 1, :] = alpha * l_ref[hd:hd + 1, :] + jnp.sum(p, axis=0, keepdims=True)
                m_ref[hd:hd + 1, :] = m_new
                part = _dot(vth[hf], p.astype(BF))
                contrib = part if contrib is None else contrib + part
                alphas.append(alpha)
            acc_ref[r] = acc_ref[r] * by_row_half(alphas[0], alphas[1]) + contrib

    def reset():
        m_ref[...] = jnp.full(m_ref.shape, NEG, F32)
        l_ref[...] = jnp.zeros_like(l_ref)
        acc_ref[...] = jnp.zeros_like(acc_ref)

    def finish():
        outs = []
        for r in range(NSA_GROUP):
            inv = by_row_half(1.0 / l_ref[2 * r:2 * r + 1, :], 1.0 / l_ref[2 * r + 1:2 * r + 2, :])
            outs.append(acc_ref[r] * inv)
        return outs

    qr_h = [halves(qr_refs[r][...]) for r in range(NSA_GROUP)]
    krow = lax.broadcasted_iota(jnp.int32, (tk, tq), 0)
    n_sub = tk // SEL_BLOCK

    reset()

    def sel_block(j, causal):
        kb = ks_ref[pl.ds(pl.multiple_of(j * tk, tk), tk), :]
        masks = []
        for hf in range(2):
            selv = jnp.concatenate(
                [jnp.broadcast_to(sel_ref[hf, pl.ds(j * n_sub + u, 1), :], (SEL_BLOCK, tq)) for u in range(n_sub)],
                axis=0)
            mk = selv > 0.5
            if causal:
                mk = mk & ((j * tk + krow) <= qpos)
            masks.append(mk)
        attend(qr_h, kb, vst_ref[j], masks)

    def sel_body(j, carry):
        sel_block(j, False)
        return carry

    lax.fori_loop(0, i, sel_body, 0)
    sel_block(i, True)
    o_sel = finish()

    reset()

    def win_body(j, carry):
        kpos = j * tk + krow
        mk = (kpos <= qpos) & (kpos > qpos - WINDOW)
        attend(qr_h, kw_ref[pl.ds(pl.multiple_of(j * tk, tk), tk), :], vwt_ref[j], (mk, mk))
        return carry

    lax.fori_loop(jnp.maximum(i - WINDOW // tk, 0), i + 1, win_body, 0)
    o_win = finish()

    gt = _sigmoid(gate_ref[...]).T
    for r in range(NSA_GROUP):
        def gate(branch):
            c = branch * 2 * NSA_GROUP + 2 * r
            return by_row_half(gt[c:c + 1, :], gt[c + 1:c + 2, :])
        o = gate(0) * occ_ref[r] + gate(1) * o_sel[r] + gate(2) * o_win[r]
        o_refs[r][...] = o.T.astype(o_refs[r].dtype)


def _nsa_attn(qn, qr, kc_bf, vct_bf, ks_bf, vst3, kw_bf, vwt3, gates, *, batch, seq):
    m = qn[0].shape[0]
    tq = KEY_BLOCK
    nq = seq // tq
    n_cmp = seq // CMP_BLOCK
    n_sel = seq // SEL_BLOCK
    qspec = pl.BlockSpec((tq, LANES), lambda b, p, i: (b * nq + i, p))
    kspec = pl.BlockSpec((seq, LANES), lambda b, p, i: (b, p))
    vspec = pl.BlockSpec((nq, LANES, tq), lambda b, p, i: (b, p, 0))
    return pl.pallas_call(
        _nsa_attn_kernel,
        grid=(batch, KV_WIDTH // LANES, nq),
        in_specs=[qspec] * 6 + [pl.BlockSpec((n_cmp, LANES), lambda b, p, i: (b, p)),
                                pl.BlockSpec((LANES, n_cmp), lambda b, p, i: (p, b)),
                                kspec, vspec, kspec, vspec, qspec],
        out_specs=[qspec] * 3,
        out_shape=[jax.ShapeDtypeStruct((m, KV_WIDTH), BF)] * 3,
        scratch_shapes=[pltpu.VMEM((tq // LANES, n_cmp, LANES), F32), pltpu.VMEM((n_sel, tq), F32),
                        pltpu.VMEM((2, n_sel, tq), F32), pltpu.VMEM((8, tq), F32), pltpu.VMEM((8, tq), F32),
                        pltpu.VMEM((NSA_GROUP, LANES, tq), F32), pltpu.VMEM((NSA_GROUP, LANES, tq), F32)],
        compiler_params=_cp("parallel", "parallel", "arbitrary"),
        name="nsa_attn",
    )(*qn, *qr, kc_bf, vct_bf, ks_bf, vst3, kw_bf, vwt3, gates)


def _tile_rows(x, reps):
    return jnp.concatenate([x] * reps, axis=0)


def _block_diag_q(q, n_heads):
    t = q.shape[0]
    assert n_heads * t <= LANES and q.shape[1] == n_heads * HEAD_DIM
    rows = _tile_rows(q, LANES // t)
    rh = jnp.right_shift(lax.broadcasted_iota(jnp.int32, rows.shape, 0), t.bit_length() - 1)
    lh = _head_of(lax.broadcasted_iota(jnp.int32, rows.shape, 1))
    return jnp.where(rh == lh, rows, 0.0).astype(BF)


def _gather_diag(o, n_heads, t):
    lh = _head_of(lax.broadcasted_iota(jnp.int32, (t, o.shape[1]), 1))
    out = jnp.zeros((t, o.shape[1]), F32)
    for h in range(n_heads):
        out = jnp.where(lh == h, o[h * t:(h + 1) * t, :], out)
    return out


def _head_pairs(ref, r0=0, rows=None):
    rows = ref.shape[0] if rows is None else rows
    xt = pltpu.einshape("khd->hkd", ref[r0:r0 + rows])
    return [jnp.concatenate([xt[h], xt[h + 1]], axis=1) for h in range(0, xt.shape[0], 2)]


def _dense_rows(ref, r0=0, rows=None):
    return jnp.concatenate(_head_pairs(ref, r0, rows), axis=1)


def _pad_rows(pad_ref, x):
    pad_ref[...] = jnp.zeros_like(pad_ref)
    pad_ref[0:x.shape[0], :] = x
    return pad_ref[...].astype(BF)


def _sb_dec_kernel(pps, n_steps, pt_ref, q_ref, kn_ref, vn_ref, *rest):
    k_refs = rest[:pps]
    v_refs = rest[pps:2 * pps]
    u_ref, o_ref, qbd_ref, acc_ref, carry_ref, padk_ref, padv_ref = rest[2 * pps:]
    del pt_ref
    s = pl.program_id(1)
    t = q_ref.shape[0]
    u = u_ref[...]

    def block(kb, vb, new):
        y = _dot_nt(kb, qbd_ref[...])
        lk = _sb_log_keep(y)
        if new:
            kr = lax.broadcasted_iota(jnp.int32, y.shape, 0)
            tq = jnp.bitwise_and(lax.broadcasted_iota(jnp.int32, y.shape, 1), t - 1)
            mask = kr < tq
            lk = jnp.where(mask, lk, 0.0)
        after = _dot(u, lk.astype(BF))
        c = carry_ref[0:1, :]
        a = jnp.exp2((lk - y) + after + c)
        if new:
            a = jnp.where(mask, a, 0.0)
        carry_ref[0:1, :] = c + after[0:1, :] + lk[0:1, :]
        acc_ref[...] += _dot_tn(a.astype(BF), vb)

    @pl.when(s == 0)
    def _():
        qbd_ref[...] = _block_diag_q(q_ref[...], N_SELF_HEADS)
        acc_ref[...] = jnp.zeros_like(acc_ref)
        carry_ref[...] = jnp.zeros_like(carry_ref)
        block(_pad_rows(padk_ref, kn_ref[...]), _pad_rows(padv_ref, vn_ref[...]), True)

    for j in range(pps):
        block(_dense_rows(k_refs[j]).astype(BF), _dense_rows(v_refs[j]).astype(BF), False)

    @pl.when(s == n_steps - 1)
    def _():
        o_ref[...] = _gather_diag(acc_ref[...], N_SELF_HEADS, t)


def _sb_decode(pt, q, k_new, v_new, pool_k, pool_v, u_bf, *, layer, n_pages, pps):
    m = q.shape[0]
    t = m // (pt.shape[0] // n_pages)
    db = m // t
    n_steps = n_pages // pps
    row = pl.BlockSpec((t, SELF_WIDTH), lambda b, s, pt: (b, 0))

    def page_spec(j):
        return pl.BlockSpec((None, None, PAGE_SIZE, N_SELF_HEADS, HEAD_DIM),
                            lambda b, s, pt: (layer, pt[b * n_pages + n_pages - 1 - (s * pps + j)], 0, 0, 0))

    pages = [page_spec(j) for j in range(pps)]
    return pl.pallas_call(
        functools.partial(_sb_dec_kernel, pps, n_steps),
        grid_spec=pltpu.PrefetchScalarGridSpec(
            num_scalar_prefetch=1,
            grid=(db, n_steps),
            in_specs=[row, row, row] + pages + pages + [pl.BlockSpec((LANES, LANES), lambda b, s, pt: (0, 0))],
            out_specs=row,
            scratch_shapes=[pltpu.VMEM((LANES, SELF_WIDTH), BF), pltpu.VMEM((LANES, SELF_WIDTH), F32),
                            pltpu.VMEM((8, LANES), F32), pltpu.VMEM((LANES, SELF_WIDTH), F32),
                            pltpu.VMEM((LANES, SELF_WIDTH), F32)]),
        out_shape=jax.ShapeDtypeStruct((m, SELF_WIDTH), F32),
        compiler_params=_cp("parallel", "arbitrary"),
        name="sb_decode",
    )(pt, q, k_new, v_new, *([pool_k] * pps), *([pool_v] * pps), u_bf)


def _mem_dec_kernel(q_ref, k_ref, v_ref, o_ref):
    t = q_ref.shape[0]
    qbd = _block_diag_q(q_ref[...], N_MEM_HEADS)
    s = _dot_nt(_dense_rows(k_ref).astype(BF), qbd)
    e = jnp.exp2(s - jnp.max(s, axis=0, keepdims=True))
    p = e / jnp.sum(e, axis=0, keepdims=True)
    o_ref[...] = _gather_diag(_dot_tn(p.astype(BF), _dense_rows(v_ref).astype(BF)), N_MEM_HEADS, t)


def _mem_decode(qm, cache_k, cache_v, *, layer, t):
    m = qm.shape[0]
    row = pl.BlockSpec((t, MEM_WIDTH), lambda b: (b, 0))
    cache = pl.BlockSpec((None, None, N_MEM, N_MEM_HEADS, HEAD_DIM), lambda b: (layer, b, 0, 0, 0))
    return pl.pallas_call(
        _mem_dec_kernel,
        grid=(m // t,),
        in_specs=[row, cache, cache],
        out_specs=row,
        out_shape=jax.ShapeDtypeStruct((m, MEM_WIDTH), F32),
        compiler_params=_cp("parallel"),
        name="mem_decode",
    )(qm, cache_k, cache_v)


def _cmp_dec_kernel(n_pages, pt_ref, *refs):
    kp = refs[:n_pages]
    vp = refs[n_pages:2 * n_pages]
    pek_ref, pev_ref, wk_ref, wv_ref, ok_ref, ov_ref, xa_ref, xb_ref = refs[2 * n_pages:]
    del pt_ref
    rows = kp[0].shape[0]
    for pages, pe_ref, w_ref, o_ref in ((kp, pek_ref, wk_ref, ok_ref), (vp, pev_ref, wv_ref, ov_ref)):
        for p in range(n_pages):
            xa_ref[p * rows:(p + 1) * rows, :], xb_ref[p * rows:(p + 1) * rows, :] = _head_pairs(pages[p])
        o_ref[...] = _compress_rows(xa_ref, xb_ref, pe_ref, w_ref, o_ref.shape[0]).astype(o_ref.dtype)


def _cmp_decode(pt, pool_k, pool_v, pek2, pev2, wk2, wv2, *, layer, n_pages, db):
    n_cmp = n_pages * PAGE_SIZE // CMP_BLOCK

    def page_spec(p):
        return pl.BlockSpec((None, None, PAGE_SIZE, NSA_KV_HEADS, HEAD_DIM),
                            lambda b, pt: (layer, pt[b * n_pages + p], 0, 0, 0))

    pages = [page_spec(p) for p in range(n_pages)]
    const2 = lambda b, pt: (0, 0)
    const3 = lambda b, pt: (0, 0, 0)
    out = pl.BlockSpec((None, n_cmp, KV_WIDTH), lambda b, pt: (b, 0, 0))
    return pl.pallas_call(
        functools.partial(_cmp_dec_kernel, n_pages),
        grid_spec=pltpu.PrefetchScalarGridSpec(
            num_scalar_prefetch=1,
            grid=(db,),
            in_specs=pages + pages + [pl.BlockSpec(pek2.shape, const2), pl.BlockSpec(pev2.shape, const2),
                                      pl.BlockSpec(wk2.shape, const3), pl.BlockSpec(wv2.shape, const3)],
            out_specs=[out, out],
            scratch_shapes=[pltpu.VMEM((n_pages * PAGE_SIZE, LANES), F32)] * 2),
        out_shape=[jax.ShapeDtypeStruct((db, n_cmp, KV_WIDTH), BF)] * 2,
        compiler_params=_cp("parallel"),
        name="cmp_decode",
    )(pt, *([pool_k] * n_pages), *([pool_v] * n_pages), pek2, pev2, wk2, wv2)


def _nsa_dec_kernel(n_pages, past, pt_ref, qn0, qn1, qn2, qr0, qr1, qr2, gate_ref, kc_ref, vc_ref,
                    ksn_ref, vsn_ref, kwn_ref, vwn_ref, wk_ref, wv_ref, *rest):
    kp = rest[:n_pages]
    vp = rest[n_pages:2 * n_pages]
    o0, o1, o2, imp_ref, score_ref, sel_ref, ssel_ref, swin_ref, pada_ref, padb_ref = rest[2 * n_pages:]
    del pt_ref
    t = qn0.shape[0]
    t_bits = t.bit_length() - 1
    lane = lax.broadcasted_iota(jnp.int32, (1, LANES), 1)
    tq = jnp.bitwise_and(lane, t - 1)
    qpos = past + tq
    live = lane < N_SELF_HEADS * t

    def build(q_refs):
        parts = []
        for r in range(NSA_GROUP):
            rows = _tile_rows(q_refs[r][...], NSA_KV_HEADS)
            rg = jnp.right_shift(lax.broadcasted_iota(jnp.int32, rows.shape, 0), t_bits)
            lg = _head_of(lax.broadcasted_iota(jnp.int32, rows.shape, 1))
            parts.append(jnp.where(rg == lg, rows, 0.0))
        parts.append(jnp.zeros((LANES - N_SELF_HEADS * t, KV_WIDTH), F32))
        return jnp.concatenate(parts, axis=0).astype(BF)

    qbn = build((qn0, qn1, qn2))
    qbr = build((qr0, qr1, qr2))

    kc = kc_ref[...]
    n_cmp = kc.shape[0]
    nrow = lax.broadcasted_iota(jnp.int32, (n_cmp, LANES), 0)
    c_mask = (nrow * CMP_BLOCK + (CMP_BLOCK - 1)) <= qpos
    s = jnp.where(c_mask, _dot_nt(kc, qbn), NEG)
    e = jnp.exp2(s - jnp.max(s, axis=0, keepdims=True))
    p = jnp.where(c_mask & live, e / jnp.sum(e, axis=0, keepdims=True), 0.0)
    o_cmp = _dot_tn(p.astype(BF), vc_ref[...])

    grp = NSA_KV_HEADS * t
    imp_ref[...] = p + pltpu.roll(p, grp, axis=1) + pltpu.roll(p, 2 * grp, axis=1) + pltpu.roll(p, 3 * grp, axis=1)
    ratio = SEL_BLOCK // CMP_BLOCK
    n_pair = n_cmp // ratio
    pair = imp_ref[pl.ds(0, n_pair, stride=ratio), :] + imp_ref[pl.ds(1, n_pair, stride=ratio), :]
    n_pad = score_ref.shape[0]
    blk_imp = jnp.concatenate([pair, jnp.zeros((n_pad - n_pair, LANES), F32)], axis=0)
    blk = lax.broadcasted_iota(jnp.int32, (n_pad, LANES), 0)
    cur = _head_of(qpos)
    forced = (blk == 0) | (blk == cur) | (blk == cur - 1)
    score_ref[...] = jnp.where(forced, jnp.inf, jnp.where(blk <= cur, blk_imp, -jnp.inf))
    sel_ref[...] = _topk_select(score_ref, n_pad, cur)

    def softmax_av(s_ref, n_blocks, v_of):
        e = jnp.exp2(s_ref[...] - jnp.max(s_ref[...], axis=0, keepdims=True))
        s_ref[...] = e / jnp.sum(e, axis=0, keepdims=True)
        acc = jnp.zeros((LANES, KV_WIDTH), F32)
        for j in range(n_blocks):
            acc = acc + _dot_tn(s_ref[j * LANES:(j + 1) * LANES, :].astype(BF), v_of(j))
        return acc

    krow = lax.broadcasted_iota(jnp.int32, (LANES, LANES), 0)
    per_page = PAGE_SIZE // SEL_BLOCK

    for pg in range(n_pages):
        sc = _dot_nt(_dense_rows(kp[pg]).astype(BF), qbr)
        selv = jnp.concatenate(
            [jnp.broadcast_to(sel_ref[pg * per_page + u:pg * per_page + u + 1, :], (SEL_BLOCK, LANES))
             for u in range(per_page)], axis=0)
        ssel_ref[pg * LANES:(pg + 1) * LANES, :] = jnp.where(selv > 0.5, sc, NEG)
    new_blk = n_pages * per_page
    sc = _dot_nt(_pad_rows(pada_ref, ksn_ref[...]), qbr)
    ok = (jnp.broadcast_to(sel_ref[new_blk:new_blk + 1, :], (LANES, LANES)) > 0.5) & (krow <= tq) & (krow < t)
    ssel_ref[n_pages * LANES:(n_pages + 1) * LANES, :] = jnp.where(ok, sc, NEG)
    vs_new = _pad_rows(padb_ref, vsn_ref[...])
    o_sel = softmax_av(ssel_ref, n_pages + 1,
                       lambda j: vs_new if j == n_pages else _dense_rows(vp[j]).astype(BF))

    n_win = wk_ref.shape[0]
    w0 = past - n_win
    wrow = lax.broadcasted_iota(jnp.int32, (n_win, LANES), 0)
    wpos = w0 + wrow
    sc = _dot_nt(_dense_rows(wk_ref).astype(BF), qbr)
    swin_ref[0:n_win, :] = jnp.where((wpos <= qpos) & (wpos > qpos - WINDOW), sc, NEG)
    sc = _dot_nt(_pad_rows(pada_ref, kwn_ref[...]), qbr)
    swin_ref[n_win:n_win + LANES, :] = jnp.where((krow <= tq) & (krow < t), sc, NEG)
    vw_new = _pad_rows(padb_ref, vwn_ref[...])
    n_wblk = n_win // LANES
    o_win = softmax_av(swin_ref, n_wblk + 1,
                       lambda j: vw_new if j == n_wblk else _dense_rows(wv_ref, j * LANES, LANES).astype(BF))

    gates = _tile_rows(_sigmoid(gate_ref[...]), LANES // t)
    row = lax.broadcasted_iota(jnp.int32, gates.shape, 0)
    col = lax.broadcasted_iota(jnp.int32, gates.shape, 1)
    r_of = jnp.right_shift(row, t_bits + 2)
    g_of = jnp.bitwise_and(jnp.right_shift(row, t_bits), NSA_KV_HEADS - 1)
    base = jnp.right_shift(g_of, 1) * LANES + r_of * 2 + jnp.bitwise_and(g_of, 1)

    def gate(branch):
        return jnp.sum(jnp.where(col == base + branch * 2 * NSA_GROUP, gates, 0.0), axis=1, keepdims=True)

    o = gate(0) * o_cmp + gate(1) * o_sel + gate(2) * o_win
    for r, o_ref in enumerate((o0, o1, o2)):
        o_ref[...] = _gather_diag(o[r * grp:(r + 1) * grp, :], NSA_KV_HEADS, t)


def _nsa_decode(pt, qn, qr, gates, kc, vc, new_rows, win_k, win_v, pool_k, pool_v, *, layer, n_pages, t):
    m = qn[0].shape[0]
    db = m // t
    past = n_pages * PAGE_SIZE
    n_cmp = kc.shape[1]
    n_win = win_k.shape[2]
    n_sel_pad = -(-(-(-(past + t) // SEL_BLOCK)) // 8) * 8
    row = pl.BlockSpec((t, KV_WIDTH), lambda b, pt: (b, 0))
    cmp_spec = pl.BlockSpec((None, n_cmp, KV_WIDTH), lambda b, pt: (b, 0, 0))
    win_spec = pl.BlockSpec((None, None, n_win, NSA_KV_HEADS, HEAD_DIM), lambda b, pt: (layer, b, 0, 0, 0))

    def page_spec(p):
        return pl.BlockSpec((None, None, PAGE_SIZE, NSA_KV_HEADS, HEAD_DIM),
                            lambda b, pt: (layer, pt[b * n_pages + p], 0, 0, 0))

    pages = [page_spec(p) for p in range(n_pages)]
    return pl.pallas_call(
        functools.partial(_nsa_dec_kernel, n_pages, past),
        grid_spec=pltpu.PrefetchScalarGridSpec(
            num_scalar_prefetch=1,
            grid=(db,),
            in_specs=[row] * 7 + [cmp_spec, cmp_spec] + [row] * 4 + [win_spec, win_spec] + pages + pages,
            out_specs=[row] * 3,
            scratch_shapes=[pltpu.VMEM((n_cmp, LANES), F32), pltpu.VMEM((n_sel_pad, LANES), F32),
                            pltpu.VMEM((n_sel_pad, LANES), F32), pltpu.VMEM(((n_pages + 1) * LANES, LANES), F32),
                            pltpu.VMEM((n_win + LANES, LANES), F32), pltpu.VMEM((LANES, KV_WIDTH), F32),
                            pltpu.VMEM((LANES, KV_WIDTH), F32)]),
        out_shape=[jax.ShapeDtypeStruct((m, KV_WIDTH), F32)] * 3,
        compiler_params=_cp("parallel"),
        name="nsa_decode",
    )(pt, *qn, *qr, gates, kc, vc, *new_rows, win_k, win_v, *([pool_k] * n_pages), *([pool_v] * n_pages))


NSA_IN = SELF_WIDTH + 6 * KV_WIDTH + 3 * N_SELF_HEADS + MEM_WIDTH
ROW_TILE = 512
SB_PAGES_PER_STEP = 4


def _nsa_head_order():
    return [NSA_GROUP * g + r for r in range(NSA_GROUP) for g in range(NSA_KV_HEADS)]


def _nsa_in_columns():
    q = [h * HEAD_DIM + d for h in _nsa_head_order() for d in range(HEAD_DIM)]
    kv = list(range(SELF_WIDTH, SELF_WIDTH + 6 * KV_WIDTH))
    g0 = SELF_WIDTH + 6 * KV_WIDTH
    gate = []
    for gp in range(KV_WIDTH // LANES):
        for c in range(LANES):
            if c < 3 * 2 * NSA_GROUP:
                branch, r, hf = c // (2 * NSA_GROUP), (c % (2 * NSA_GROUP)) // 2, c % 2
                gate.append(g0 + (NSA_GROUP * (2 * gp + hf) + r) * 3 + branch)
            else:
                gate.append(NSA_IN)
    qm = list(range(g0 + 3 * N_SELF_HEADS, NSA_IN))
    cols = np.asarray(q + kv + gate + qm, np.int32)
    assert cols.shape[0] == _NSA_COLS
    return cols


def _nsa_out_rows():
    o = [h * HEAD_DIM + d for h in _nsa_head_order() for d in range(HEAD_DIM)]
    return np.asarray(o + list(range(SELF_WIDTH, SELF_WIDTH + MEM_WIDTH)), np.int32)


def _block_diag2(w):
    eye = jnp.eye(LANES // HEAD_DIM, dtype=w.dtype)
    return jnp.einsum("gh,lde->lgdhe", eye, w).reshape(w.shape[0], LANES, LANES).astype(BF)


def kernel(x_prompt, x_sample, mem_prompt, cache_sb_k, cache_sb_v, cache_nsa_cmp_k, cache_nsa_cmp_v, cache_nsa_sel_k, cache_nsa_sel_v, cache_nsa_win_k, cache_nsa_win_v, cache_mem_k, cache_mem_v, page_table, ln_mix_pre, ln_mix_post, ln_ffn_pre, ln_ffn_post, ln_mem, w_in_a, w_in_b, w_cmp_k, w_cmp_v, pe_cmp_k, pe_cmp_v, w_out, w_mem_kv, w_gate_up, w_down):
    bsz, seq, d = x_prompt.shape
    db, ds, _ = x_sample.shape
    n_pages = page_table.shape[1]
    past = n_pages * PAGE_SIZE
    depth = w_out.shape[0]
    n_win = cache_nsa_win_k.shape[2]
    tm_p = min(ROW_TILE, bsz * seq)
    tm_s = min(ROW_TILE, db * ds)
    assert d == D_MODEL and seq % KEY_BLOCK == 0 and seq % tm_p == 0 and (db * ds) % tm_s == 0 and tm_s % ds == 0
    assert ds & (ds - 1) == 0 and N_SELF_HEADS * ds <= LANES and ds < CMP_BLOCK
    assert past % SEL_BLOCK == 0 and past + ds <= past + SEL_BLOCK and n_win % LANES == 0 and n_win <= past
    assert n_pages % SB_PAGES_PER_STEP == 0 and seq >= WINDOW

    h_p = x_prompt.reshape(bsz * seq, d)
    h_s = x_sample.reshape(db * ds, d)
    mem = mem_prompt.reshape(bsz * N_MEM, d)
    pt = page_table.reshape(-1).astype(jnp.int32)
    u256 = jnp.triu(jnp.ones((KEY_BLOCK, KEY_BLOCK), BF), 1)
    u128 = jnp.triu(jnp.ones((LANES, LANES), BF), 1)
    rope_p = _rope_tables(jnp.arange(seq, dtype=jnp.int32))
    rope_s = _rope_tables(past + jnp.arange(tm_s, dtype=jnp.int32) % ds)

    nsa_cols = _nsa_in_columns()
    nsa_rows = _nsa_out_rows()

    sb_p, sb_s = [[], []], [[], []]
    nsa_p, nsa_s = [[] for _ in range(6)], [[] for _ in range(6)]
    memk_p, memv_p = [], []
    for i in range(depth):
        j = i // 2
        g_pre = ln_mix_pre[i][None]
        tail_w = (ln_mix_post[i][None], ln_ffn_pre[i][None], ln_ffn_post[i][None],
                  w_gate_up[i].astype(BF), w_down[i].astype(BF))
        wkv = w_mem_kv[i]
        mk, mv, mk_bf, mvt_bf = _memkv(mem, ln_mem[i][None], wkv.astype(BF), wkv[:, MEM_WIDTH:].T.astype(BF))
        memk_p.append(mk.reshape(bsz, N_MEM, N_MEM_HEADS, HEAD_DIM))
        memv_p.append(mv.reshape(bsz, N_MEM, N_MEM_HEADS, HEAD_DIM))
        if i % 2 == 0:
            w = w_in_a[j]
            w_bf = w.astype(BF)
            wvt = w[:, 2 * SELF_WIDTH:3 * SELF_WIDTH].T.astype(BF)
            q, k, v, kb, qm, vt3 = _sb_inproj(h_p, g_pre, w_bf, wvt, tm=tm_p, with_vt=True, q_dtype=BF)
            o = _sb_attn(q, kb, vt3, u256, batch=bsz, seq=seq)
            om = _mem_attn(qm, mk_bf, mvt_bf, batch=bsz, seq=seq)
            qs, ks, vs, _, qms = _sb_inproj(h_s, g_pre, w_bf, wvt, tm=tm_s, with_vt=False, q_dtype=F32)
            o_s = _sb_decode(pt, qs, ks, vs, cache_sb_k, cache_sb_v, u128, layer=j, n_pages=n_pages,
                             pps=SB_PAGES_PER_STEP)
            om_s = _mem_decode(qms, cache_mem_k, cache_mem_v, layer=i, t=ds)
            wo = w_out[i].astype(BF)
            mix_p, mix_s = [o, om], [o_s, om_s]
            for lst, a, shape in ((sb_p, (k, v), (bsz, seq)), (sb_s, (ks, vs), (db, ds))):
                for n in range(2):
                    lst[n].append(a[n].reshape(*shape, N_SELF_HEADS, HEAD_DIM))
        else:
            w = w_in_b[j]
            w_bf = jnp.concatenate([w, jnp.zeros((d, 1), w.dtype)], axis=1)[:, nsa_cols].astype(BF)
            vs0 = SELF_WIDTH + 3 * KV_WIDTH
            vw0 = SELF_WIDTH + 5 * KV_WIDTH
            wt = jnp.concatenate([w[:, vs0:vs0 + KV_WIDTH], w[:, vw0:vw0 + KV_WIDTH]], axis=1).T.astype(BF)
            pek2 = jnp.tile(pe_cmp_k[j], (1, LANES // HEAD_DIM))
            pev2 = jnp.tile(pe_cmp_v[j], (1, LANES // HEAD_DIM))
            wck2 = _block_diag2(w_cmp_k[j])
            wcv2 = _block_diag2(w_cmp_v[j])
            outs = _nsa_inproj(h_p, g_pre, w_bf, wt, *rope_p, tm=tm_p, with_vt=True, q_dtype=BF)
            qn, qr, rows_p = outs[0:3], outs[3:6], outs[6:12]
            ksb, kwb, gates, qm, vst3, vwt3 = outs[12:18]
            kc_bf, _ = _compress(rows_p[0], pek2, wck2, rows=seq)
            _, vct_bf = _compress(rows_p[1], pev2, wcv2, rows=seq)
            o3 = _nsa_attn(qn, qr, kc_bf, vct_bf, ksb, vst3, kwb, vwt3, gates, batch=bsz, seq=seq)
            om = _mem_attn(qm, mk_bf, mvt_bf, batch=bsz, seq=seq)
            outs = _nsa_inproj(h_s, g_pre, w_bf, wt, *rope_s, tm=tm_s, with_vt=False, q_dtype=F32)
            qn_s, qr_s, rows_s = outs[0:3], outs[3:6], outs[6:12]
            gates_s, qms = outs[14], outs[15]
            kc_s, vc_s = _cmp_decode(pt, cache_nsa_cmp_k, cache_nsa_cmp_v, pek2, pev2, wck2, wcv2, layer=j,
                                     n_pages=n_pages, db=db)
            o3_s = _nsa_decode(pt, qn_s, qr_s, gates_s, kc_s, vc_s, rows_s[2:6], cache_nsa_win_k, cache_nsa_win_v,
                               cache_nsa_sel_k, cache_nsa_sel_v, layer=j, n_pages=n_pages, t=ds)
            om_s = _mem_decode(qms, cache_mem_k, cache_mem_v, layer=i, t=ds)
            wo = w_out[i][nsa_rows].astype(BF)
            mix_p, mix_s = [*o3, om], [*o3_s, om_s]
            for n in range(6):
                rp = rows_p[n].reshape(bsz, seq, NSA_KV_HEADS, HEAD_DIM)
                nsa_p[n].append(rp[:, seq - min(WINDOW, seq):] if n >= 4 else rp)
                nsa_s[n].append(rows_s[n].reshape(db, ds, NSA_KV_HEADS, HEAD_DIM))
        h_p = _layer_tail(h_p, mix_p, wo, *tail_w, tm=tm_p)
        h_s = _layer_tail(h_s, mix_s, wo, *tail_w, tm=tm_s)

    st = lambda rows: jnp.stack(rows, axis=0)
    return (h_p.reshape(bsz, seq, d), h_s.reshape(db, ds, d),
            st(sb_p[0]), st(sb_p[1]),
            *[st(x) for x in nsa_p],
            st(memk_p), st(memv_p),
            st(sb_s[0]), st(sb_s[1]),
            *[st(x) for x in nsa_s])
```

```python
import functools

import jax
import jax.numpy as jnp
import numpy as np
from jax import lax
from jax.experimental import pallas as pl
from jax.experimental.pallas import tpu as pltpu

F32 = jnp.float32
BF = jnp.bfloat16

D_MODEL = 1024
HEAD_DIM = 64
N_SELF_HEADS = 12
N_MEM_HEADS = 4
SELF_WIDTH = N_SELF_HEADS * HEAD_DIM
MEM_WIDTH = N_MEM_HEADS * HEAD_DIM
N_MEM = 256
NSA_KV_HEADS = 4
NSA_GROUP = N_SELF_HEADS // NSA_KV_HEADS
KV_WIDTH = NSA_KV_HEADS * HEAD_DIM
CMP_BLOCK = 32
SEL_BLOCK = 64
SEL_TOPK = 16
WINDOW = 512
PAGE_SIZE = 128
ROPE_THETA = 500000.0
ROT_DIM = HEAD_DIM // 4
EPS = 1e-6
NEG = -1e30
SCALE = HEAD_DIM ** -0.5
Q_SCALE = SCALE * 1.4426950408889634

LANES = 128
KEY_BLOCK = 256
VMEM_LIMIT = 56 * 1024 * 1024


def _cp(*sem):
    return pltpu.CompilerParams(dimension_semantics=sem, vmem_limit_bytes=VMEM_LIMIT)


def _dot(a, b):
    return jnp.dot(a, b, preferred_element_type=F32)


def _dot_nt(a, b):
    return lax.dot_general(a, b, (((1,), (1,)), ((), ())), preferred_element_type=F32)


def _dot_tn(a, b):
    return lax.dot_general(a, b, (((0,), (0,)), ((), ())), preferred_element_type=F32)


def _rms(x, g):
    return x * lax.rsqrt(jnp.mean(x * x, axis=-1, keepdims=True) + EPS) * g


def _sigmoid(x):
    return 1.0 / (1.0 + jnp.exp(-x))


def _neg_abs(y):
    bits = lax.bitcast_convert_type(y, jnp.int32) | jnp.int32(-2 ** 31)
    return lax.bitcast_convert_type(bits, F32)


def _sb_log_keep(y):
    return jnp.minimum(y, 0.0) - jnp.log2(1.0 + jnp.exp2(_neg_abs(y)))


def _head_of(idx):
    return jnp.right_shift(idx, 6)


def _const_spec(shape):
    nd = len(shape)
    return pl.BlockSpec(shape, lambda *_: (0,) * nd, pipeline_mode=pl.Buffered(1))


def _rope128(x, c, s):
    lane = lax.broadcasted_iota(jnp.int32, x.shape, 1)
    up = pltpu.roll(x, LANES - ROT_DIM // 2, axis=1)
    dn = pltpu.roll(x, ROT_DIM // 2, axis=1)
    sw = jnp.where(jnp.bitwise_and(lane, HEAD_DIM - 1) < ROT_DIM // 2, up, dn)
    return x * c + sw * s


def _rope_tables(pos):
    half = ROT_DIM // 2
    inv = ROPE_THETA ** (-jnp.arange(half, dtype=F32) * 2.0 / ROT_DIM)
    ang = pos.astype(F32)[:, None] * inv[None, :]
    cos, sin = jnp.cos(ang), jnp.sin(ang)
    n = pos.shape[0]
    c64 = jnp.concatenate([cos, cos, jnp.ones((n, HEAD_DIM - ROT_DIM), F32)], axis=1)
    s64 = jnp.concatenate([-sin, sin, jnp.zeros((n, HEAD_DIM - ROT_DIM), F32)], axis=1)
    return jnp.tile(c64, (1, LANES // HEAD_DIM)), jnp.tile(s64, (1, LANES // HEAD_DIM))


def _sb_inproj_kernel(x_ref, g_ref, w_ref, wvt_ref, q_ref, k_ref, v_ref, kb_ref, qm_ref, *vt_refs):
    xn = _rms(x_ref[...], g_ref[...]).astype(BF)
    q_ref[...] = (_dot(xn, w_ref[:, 0:SELF_WIDTH]) * -Q_SCALE).astype(q_ref.dtype)
    k = _dot(xn, w_ref[:, SELF_WIDTH:2 * SELF_WIDTH])
    k_ref[...] = k
    kb_ref[...] = k.astype(BF)
    v_ref[...] = _dot(xn, w_ref[:, 2 * SELF_WIDTH:3 * SELF_WIDTH])
    qm_ref[...] = (_dot(xn, w_ref[:, 3 * SELF_WIDTH:3 * SELF_WIDTH + MEM_WIDTH]) * Q_SCALE).astype(qm_ref.dtype)
    if vt_refs:
        (vt_ref,) = vt_refs
        vt = _dot_nt(wvt_ref[...], xn)
        for c in range(vt_ref.shape[0]):
            vt_ref[c] = vt[:, c * KEY_BLOCK:(c + 1) * KEY_BLOCK].astype(BF)


def _sb_inproj(x, g, w_bf, wvt_bf, *, tm, with_vt, q_dtype):
    m = x.shape[0]
    grid = (m // tm,)
    row = lambda i: (i, 0)
    const = lambda i: (0, 0)
    out_shape = [jax.ShapeDtypeStruct((m, SELF_WIDTH), q_dtype),
                 jax.ShapeDtypeStruct((m, SELF_WIDTH), F32),
                 jax.ShapeDtypeStruct((m, SELF_WIDTH), F32),
                 jax.ShapeDtypeStruct((m, SELF_WIDTH), BF),
                 jax.ShapeDtypeStruct((m, MEM_WIDTH), q_dtype)]
    out_specs = [pl.BlockSpec((tm, SELF_WIDTH), row)] * 4 + [pl.BlockSpec((tm, MEM_WIDTH), row)]
    if with_vt:
        out_shape.append(jax.ShapeDtypeStruct((m // KEY_BLOCK, SELF_WIDTH, KEY_BLOCK), BF))
        out_specs.append(pl.BlockSpec((tm // KEY_BLOCK, SELF_WIDTH, KEY_BLOCK), lambda i: (i, 0, 0)))
    return pl.pallas_call(
        _sb_inproj_kernel,
        grid=grid,
        in_specs=[pl.BlockSpec((tm, D_MODEL), row), _const_spec((1, D_MODEL)),
                  _const_spec(w_bf.shape), _const_spec(wvt_bf.shape)],
        out_specs=out_specs,
        out_shape=out_shape,
        compiler_params=_cp("parallel"),
        name="sb_inproj",
    )(x, g, w_bf, wvt_bf)


_NSA_Q0 = 0
_NSA_KV0 = SELF_WIDTH
_NSA_G0 = _NSA_KV0 + 6 * KV_WIDTH
_NSA_QM0 = _NSA_G0 + 2 * LANES
_NSA_COLS = _NSA_QM0 + MEM_WIDTH


def _nsa_inproj_kernel(x_ref, g_ref, w_ref, wt_ref, c_ref, s_ref,
                       qn0, qn1, qn2, qr0, qr1, qr2, kc_ref, vc_ref, ks_ref, vs_ref, kw_ref, vw_ref,
                       ksb_ref, kwb_ref, gate_ref, qm_ref, *vt_refs):
    xn = _rms(x_ref[...], g_ref[...]).astype(BF)
    c = c_ref[...]
    s = s_ref[...]

    def rope256(y):
        return jnp.concatenate([_rope128(y[:, :LANES], c, s), _rope128(y[:, LANES:], c, s)], axis=1)

    for r, (qn_ref, qr_ref) in enumerate(((qn0, qr0), (qn1, qr1), (qn2, qr2))):
        q = _dot(xn, w_ref[:, r * KV_WIDTH:(r + 1) * KV_WIDTH]) * Q_SCALE
        qn_ref[...] = q.astype(qn_ref.dtype)
        qr_ref[...] = rope256(q).astype(qr_ref.dtype)

    def kv(j):
        return _dot(xn, w_ref[:, _NSA_KV0 + j * KV_WIDTH:_NSA_KV0 + (j + 1) * KV_WIDTH])

    kc_ref[...] = kv(0)
    vc_ref[...] = kv(1)
    ks = rope256(kv(2))
    ks_ref[...] = ks
    ksb_ref[...] = ks.astype(BF)
    vs_ref[...] = kv(3)
    kw = rope256(kv(4))
    kw_ref[...] = kw
    kwb_ref[...] = kw.astype(BF)
    vw_ref[...] = kv(5)
    gate_ref[...] = _dot(xn, w_ref[:, _NSA_G0:_NSA_G0 + 2 * LANES])
    qm_ref[...] = (_dot(xn, w_ref[:, _NSA_QM0:_NSA_QM0 + MEM_WIDTH]) * Q_SCALE).astype(qm_ref.dtype)
    if vt_refs:
        vst_ref, vwt_ref = vt_refs
        vt = _dot_nt(wt_ref[...], xn)
        for cblk in range(vst_ref.shape[0]):
            sl = slice(cblk * KEY_BLOCK, (cblk + 1) * KEY_BLOCK)
            vst_ref[cblk] = vt[:KV_WIDTH, sl].astype(BF)
            vwt_ref[cblk] = vt[KV_WIDTH:, sl].astype(BF)


def _nsa_inproj(x, g, w_bf, wt_bf, ctab, stab, *, tm, with_vt, q_dtype):
    m = x.shape[0]
    grid = (m // tm,)
    row = lambda i: (i, 0)
    const = lambda i: (0, 0)
    tbl_blocks = ctab.shape[0] // tm
    tbl = lambda i: (i % tbl_blocks, 0)
    kvs = lambda dt: jax.ShapeDtypeStruct((m, KV_WIDTH), dt)
    out_shape = [kvs(q_dtype)] * 6 + [kvs(F32)] * 6 + [kvs(BF)] * 2 + [kvs(F32), kvs(q_dtype)]
    out_specs = [pl.BlockSpec((tm, KV_WIDTH), row)] * 16
    if with_vt:
        out_shape += [jax.ShapeDtypeStruct((m // KEY_BLOCK, KV_WIDTH, KEY_BLOCK), BF)] * 2
        out_specs += [pl.BlockSpec((tm // KEY_BLOCK, KV_WIDTH, KEY_BLOCK), lambda i: (i, 0, 0))] * 2
    return pl.pallas_call(
        _nsa_inproj_kernel,
        grid=grid,
        in_specs=[pl.BlockSpec((tm, D_MODEL), row), _const_spec((1, D_MODEL)),
                  _const_spec(w_bf.shape), _const_spec(wt_bf.shape),
                  pl.BlockSpec((tm, LANES), tbl), pl.BlockSpec((tm, LANES), tbl)],
        out_specs=out_specs,
        out_shape=out_shape,
        compiler_params=_cp("parallel"),
        name="nsa_inproj",
    )(x, g, w_bf, wt_bf, ctab, stab)


def _memkv_kernel(x_ref, g_ref, w_ref, wvt_ref, k_ref, v_ref, kb_ref, vt_ref):
    xn = _rms(x_ref[...], g_ref[...]).astype(BF)
    k = _dot(xn, w_ref[:, :MEM_WIDTH])
    k_ref[...] = k
    kb_ref[...] = k.astype(BF)
    v_ref[...] = _dot(xn, w_ref[:, MEM_WIDTH:])
    vt_ref[...] = _dot_nt(wvt_ref[...], xn).astype(BF)


def _memkv(x, g, w_bf, wvt_bf):
    m = x.shape[0]
    row = lambda i: (i, 0)
    const = lambda i: (0, 0)
    return pl.pallas_call(
        _memkv_kernel,
        grid=(m // N_MEM,),
        in_specs=[pl.BlockSpec((N_MEM, D_MODEL), row), _const_spec((1, D_MODEL)),
                  _const_spec(w_bf.shape), _const_spec(wvt_bf.shape)],
        out_specs=[pl.BlockSpec((N_MEM, MEM_WIDTH), row)] * 3 + [pl.BlockSpec((MEM_WIDTH, N_MEM), lambda i: (0, i))],
        out_shape=[jax.ShapeDtypeStruct((m, MEM_WIDTH), F32)] * 2 + [jax.ShapeDtypeStruct((m, MEM_WIDTH), BF),
                                                                     jax.ShapeDtypeStruct((MEM_WIDTH, m), BF)],
        compiler_params=_cp("parallel"),
        name="memkv",
    )(x, g, w_bf, wvt_bf)


def _tail_kernel(n_mix, *refs):
    h_ref = refs[0]
    mix_refs = refs[1:1 + n_mix]
    wo_ref, gpost_ref, gpre_ref, gfpost_ref, wgu_ref, wd_ref, o_ref = refs[1 + n_mix:]
    y = None
    off = 0
    for mref in mix_refs:
        w = mref.shape[1]
        part = _dot(mref[...].astype(BF), wo_ref[off:off + w, :])
        y = part if y is None else y + part
        off += w
    h = h_ref[...] + _rms(y, gpost_ref[...])
    xn = _rms(h, gpre_ref[...]).astype(BF)
    d_ff = wd_ref.shape[0]
    acc = None
    for c0 in range(0, d_ff, KEY_BLOCK):
        gt = _dot(xn, wgu_ref[:, c0:c0 + KEY_BLOCK])
        up = _dot(xn, wgu_ref[:, d_ff + c0:d_ff + c0 + KEY_BLOCK])
        act = (gt * _sigmoid(gt) * up).astype(BF)
        part = _dot(act, wd_ref[c0:c0 + KEY_BLOCK, :])
        acc = part if acc is None else acc + part
    o_ref[...] = h + _rms(acc, gfpost_ref[...])


def _layer_tail(h, mixes, wo_bf, g_post, g_pre, g_fpost, wgu_bf, wd_bf, *, tm):
    m = h.shape[0]
    row = lambda i: (i, 0)
    const = lambda i: (0, 0)
    return pl.pallas_call(
        functools.partial(_tail_kernel, len(mixes)),
        grid=(m // tm,),
        in_specs=[pl.BlockSpec((tm, D_MODEL), row)] + [pl.BlockSpec((tm, x.shape[1]), row) for x in mixes]
        + [_const_spec(wo_bf.shape)] + [_const_spec((1, D_MODEL))] * 3
        + [_const_spec(wgu_bf.shape), _const_spec(wd_bf.shape)],
        out_specs=pl.BlockSpec((tm, D_MODEL), row),
        out_shape=jax.ShapeDtypeStruct((m, D_MODEL), F32),
        compiler_params=_cp("parallel"),
        name="layer_tail",
    )(h, *mixes, wo_bf, g_post, g_pre, g_fpost, wgu_bf, wd_bf)


def _half_masks(shape, axis):
    idx = lax.broadcasted_iota(jnp.int32, shape, axis)
    return idx < HEAD_DIM, idx >= HEAD_DIM


def _sb_attn_kernel(q_ref, k_ref, vt_ref, u_ref, o_ref, acc_ref, carry_ref):
    tq = q_ref.shape[0]
    i = pl.program_id(2)
    qi = q_ref[...]
    lo_l, hi_l = _half_masks(qi.shape, 1)
    zero = jnp.zeros_like(qi)
    qh = (jnp.where(lo_l, qi, zero), jnp.where(hi_l, qi, zero))
    acc_ref[...] = jnp.zeros_like(acc_ref)
    carry_ref[...] = jnp.zeros_like(carry_ref)
    u = u_ref[...]

    def run(blocks):
        carry = [carry_ref[h:h + 1, :] for h in range(2)]
        staged = []
        for j, diag in blocks:
            kb = k_ref[pl.ds(pl.multiple_of(j * tq, tq), tq), :]
            vt = vt_ref[j]
            lo_r, hi_r = _half_masks(vt.shape, 0)
            zv = jnp.zeros_like(vt)
            vth = (jnp.where(lo_r, vt, zv), jnp.where(hi_r, vt, zv))
            mask = None
            if diag:
                kr = lax.broadcasted_iota(jnp.int32, (tq, tq), 0)
                qc = lax.broadcasted_iota(jnp.int32, (tq, tq), 1)
                mask = kr < qc
            for h in range(2):
                y = _dot_nt(kb, qh[h])
                lk = _sb_log_keep(y)
                if diag:
                    lk = jnp.where(mask, lk, 0.0)
                after = _dot(u, lk.astype(BF))
                staged.append((h, y, lk, after, vth[h], mask))
        contrib = None
        for h, y, lk, after, vth_h, mask in staged:
            a = jnp.exp2((lk - y) + after + carry[h])
            if mask is not None:
                a = jnp.where(mask, a, 0.0)
            carry[h] = carry[h] + after[0:1, :] + lk[0:1, :]
            part = _dot(vth_h, a.astype(BF))
            contrib = part if contrib is None else contrib + part
        acc_ref[...] += contrib
        for h in range(2):
            carry_ref[h:h + 1, :] = carry[h]

    run([(i, True)])
    odd = jnp.bitwise_and(i, 1)

    @pl.when(odd == 1)
    def _():
        run([(i - 1, False)])

    def body(jj, c):
        j = i - 1 - odd - 2 * jj
        run([(j, False), (j - 1, False)])
        return c

    lax.fori_loop(0, jnp.right_shift(i, 1), body, 0)
    o_ref[...] = acc_ref[...].T.astype(o_ref.dtype)


def _sb_attn(q_bf, k_bf, vt3, u_bf, *, batch, seq):
    m = q_bf.shape[0]
    tq = KEY_BLOCK
    nq = seq // tq
    return pl.pallas_call(
        _sb_attn_kernel,
        grid=(batch, SELF_WIDTH // LANES, nq),
        in_specs=[pl.BlockSpec((tq, LANES), lambda b, p, i: (b * nq + i, p)),
                  pl.BlockSpec((seq, LANES), lambda b, p, i: (b, p)),
                  pl.BlockSpec((nq, LANES, tq), lambda b, p, i: (b, p, 0)),
                  _const_spec((tq, tq))],
        out_specs=pl.BlockSpec((tq, LANES), lambda b, p, i: (b * nq + i, p)),
        out_shape=jax.ShapeDtypeStruct((m, SELF_WIDTH), BF),
        scratch_shapes=[pltpu.VMEM((LANES, tq), F32), pltpu.VMEM((8, tq), F32)],
        compiler_params=_cp("parallel", "parallel", "arbitrary"),
        name="sb_attn",
    )(q_bf, k_bf, vt3, u_bf)


def _mem_attn_kernel(q_ref, k_ref, vt_ref, o_ref):
    qi = q_ref[...]
    k = k_ref[...]
    vt = vt_ref[...]
    lane_h = _head_of(lax.broadcasted_iota(jnp.int32, qi.shape, 1))
    row_h = _head_of(lax.broadcasted_iota(jnp.int32, vt.shape, 0))
    acc = None
    for h in range(N_MEM_HEADS):
        s = _dot_nt(k, jnp.where(lane_h == h, qi, jnp.zeros_like(qi)))
        e = jnp.exp2(s - jnp.max(s, axis=0, keepdims=True))
        p = e / jnp.sum(e, axis=0, keepdims=True)
        part = _dot(jnp.where(row_h == h, vt, jnp.zeros_like(vt)), p.astype(BF))
        acc = part if acc is None else acc + part
    o_ref[...] = acc.T.astype(o_ref.dtype)


def _mem_attn(qm_bf, mk_bf, mvt_bf, *, batch, seq):
    m = qm_bf.shape[0]
    tq = KEY_BLOCK
    nq = seq // tq
    return pl.pallas_call(
        _mem_attn_kernel,
        grid=(batch, nq),
        in_specs=[pl.BlockSpec((tq, MEM_WIDTH), lambda b, i: (b * nq + i, 0)),
                  pl.BlockSpec((N_MEM, MEM_WIDTH), lambda b, i: (b, 0)),
                  pl.BlockSpec((MEM_WIDTH, N_MEM), lambda b, i: (0, b))],
        out_specs=pl.BlockSpec((tq, MEM_WIDTH), lambda b, i: (b * nq + i, 0)),
        out_shape=jax.ShapeDtypeStruct((m, MEM_WIDTH), BF),
        compiler_params=_cp("parallel", "parallel"),
        name="mem_attn",
    )(qm_bf, mk_bf, mvt_bf)


def _compress_rows(xa_ref, xb_ref, pe_ref, w_ref, nblk):
    acc = None
    for l in range(CMP_BLOCK):
        xs = [ref[pl.ds(l, nblk, stride=CMP_BLOCK), :] for ref in (xa_ref, xb_ref)]
        xl = jnp.concatenate(xs, axis=0) + pe_ref[l:l + 1, :]
        part = _dot(xl.astype(BF), w_ref[l])
        acc = part if acc is None else acc + part
    return jnp.concatenate([acc[:nblk], acc[nblk:]], axis=1)


def _compress_kernel(xa_ref, xb_ref, pe_ref, w_ref, o_ref, ot_ref):
    out = _compress_rows(xa_ref, xb_ref, pe_ref, w_ref, o_ref.shape[0])
    o_ref[...] = out.astype(o_ref.dtype)
    ot_ref[...] = out.T.astype(ot_ref.dtype)


def _compress(x, pe2, wbd2, *, rows):
    m = x.shape[0]
    nblk = rows // CMP_BLOCK
    return pl.pallas_call(
        _compress_kernel,
        grid=(m // rows,),
        in_specs=[pl.BlockSpec((rows, LANES), lambda i: (i, 0)), pl.BlockSpec((rows, LANES), lambda i: (i, 1)),
                  _const_spec(pe2.shape), _const_spec(wbd2.shape)],
        out_specs=[pl.BlockSpec((nblk, KV_WIDTH), lambda i: (i, 0)), pl.BlockSpec((KV_WIDTH, nblk), lambda i: (0, i))],
        out_shape=[jax.ShapeDtypeStruct((m // CMP_BLOCK, KV_WIDTH), BF),
                   jax.ShapeDtypeStruct((KV_WIDTH, m // CMP_BLOCK), BF)],
        compiler_params=_cp("parallel"),
        name="compress",
    )(x, x, pe2, wbd2)


def _topk_select(score_ref, n_blocks, cur):
    score = score_ref[...]
    blk = lax.broadcasted_iota(jnp.int32, score.shape, 0)

    def body(i, cnt):
        si = score_ref[pl.ds(i, 1), :]
        tie = jnp.where(blk > i, 1.0, 0.0)
        return cnt + jnp.where(si > score, 1.0, jnp.where(si == score, tie, 0.0))

    rank = lax.fori_loop(0, n_blocks, body, jnp.zeros(score.shape, F32))
    return jnp.where((rank < SEL_TOPK) & (blk <= cur), 1.0, 0.0).astype(F32)


def _nsa_attn_kernel(qn0, qn1, qn2, qr0, qr1, qr2, kc_ref, vct_ref, ks_ref, vst_ref, kw_ref, vwt_ref, gate_ref,
                     o0, o1, o2, imp_ref, score_ref, sel_ref, m_ref, l_ref, acc_ref, occ_ref):
    tq = qn0.shape[0]
    tk = tq
    i = pl.program_id(2)
    qn_refs = (qn0, qn1, qn2)
    qr_refs = (qr0, qr1, qr2)
    o_refs = (o0, o1, o2)
    qpos = i * tq + lax.broadcasted_iota(jnp.int32, (1, tq), 1)

    def halves(x):
        lo, hi = _half_masks(x.shape, 1)
        z = jnp.zeros_like(x)
        return jnp.where(lo, x, z), jnp.where(hi, x, z)

    def by_row_half(a, b):
        lo, _ = _half_masks((LANES, tq), 0)
        return jnp.where(lo, a, b)

    def vhalves(vt):
        lo, hi = _half_masks(vt.shape, 0)
        z = jnp.zeros_like(vt)
        return jnp.where(lo, vt, z), jnp.where(hi, vt, z)

    n_cmp = kc_ref.shape[0]
    kc = kc_ref[...]
    vct = vhalves(vct_ref[...])
    nrow = lax.broadcasted_iota(jnp.int32, (n_cmp, tq), 0)
    c_mask = (nrow * CMP_BLOCK + (CMP_BLOCK - 1)) <= qpos
    imp = [None, None]
    for r in range(NSA_GROUP):
        qh = halves(qn_refs[r][...])
        occ = None
        for hf in range(2):
            s = jnp.where(c_mask, _dot_nt(kc, qh[hf]), NEG)
            e = jnp.exp2(s - jnp.max(s, axis=0, keepdims=True))
            p = jnp.where(c_mask, e / jnp.sum(e, axis=0, keepdims=True), 0.0)
            imp[hf] = p if imp[hf] is None else imp[hf] + p
            part = _dot(vct[hf], p.astype(BF))
            occ = part if occ is None else occ + part
        occ_ref[r] = occ

    n_sel = sel_ref.shape[1]
    ratio = SEL_BLOCK // CMP_BLOCK
    blk = lax.broadcasted_iota(jnp.int32, (n_sel, tq), 0)
    cur = _head_of(qpos)
    forced = (blk == 0) | (blk == cur) | (blk == cur - 1)
    allowed = blk <= cur
    scores = []
    for hf in range(2):
        parts = []
        for c in range(tq // LANES):
            imp_ref[c] = imp[hf][:, c * LANES:(c + 1) * LANES]
            parts.append(imp_ref[c, pl.ds(0, n_sel, stride=ratio), :] + imp_ref[c, pl.ds(1, n_sel, stride=ratio), :])
        blk_imp = jnp.concatenate(parts, axis=1)
        scores.append(jnp.where(forced, jnp.inf, jnp.where(allowed, blk_imp, -jnp.inf)))
    for hf in range(2):
        score_ref[...] = scores[hf]
        sel_ref[hf] = _topk_select(score_ref, n_sel, cur)

    def attend(q_halves, kb, vt, masks):
        vth = vhalves(vt)
        for r in range(NSA_GROUP):
            contrib = None
            alphas = []
            for hf in range(2):
                hd = 2 * r + hf
                s = jnp.where(masks[hf], _dot_nt(kb, q_halves[r][hf]), NEG)
                m_old = m_ref[hd:hd + 1, :]
                m_new = jnp.maximum(m_old, jnp.max(s, axis=0, keepdims=True))
                alpha = jnp.exp2(m_old - m_new)
                p = jnp.exp2(s - m_new)
                l_ref[hd:hd + 1, :] = alpha * l_ref[hd:hd + 1, :] + jnp.sum(p, axis=0, keepdims=True)
                m_ref[hd:hd + 1, :] = m_new
                part = _dot(vth[hf], p.astype(BF))
                contrib = part if contrib is None else contrib + part
                alphas.append(alpha)
            acc_ref[r] = acc_ref[r] * by_row_half(alphas[0], alphas[1]) + contrib

    def reset():
        m_ref[...] = jnp.full(m_ref.shape, NEG, F32)
        l_ref[...] = jnp.zeros_like(l_ref)
        acc_ref[...] = jnp.zeros_like(acc_ref)

    def finish():
        outs = []
        for r in range(NSA_GROUP):
            inv = by_row_half(1.0 / l_ref[2 * r:2 * r + 1, :], 1.0 / l_ref[2 * r + 1:2 * r + 2, :])
            outs.append(acc_ref[r] * inv)
        return outs

    qr_h = [halves(qr_refs[r][...]) for r in range(NSA_GROUP)]
    krow = lax.broadcasted_iota(jnp.int32, (tk, tq), 0)
    n_sub = tk // SEL_BLOCK

    reset()

    def sel_block(j, causal):
        kb = ks_ref[pl.ds(pl.multiple_of(j * tk, tk), tk), :]
        masks = []
        for hf in range(2):
            selv = jnp.concatenate(
                [jnp.broadcast_to(sel_ref[hf, pl.ds(j * n_sub + u, 1), :], (SEL_BLOCK, tq)) for u in range(n_sub)],
                axis=0)
            mk = selv > 0.5
            if causal:
                mk = mk & ((j * tk + krow) <= qpos)
            masks.append(mk)
        attend(qr_h, kb, vst_ref[j], masks)

    def sel_body(j, carry):
        sel_block(j, False)
        return carry

    lax.fori_loop(0, i, sel_body, 0)
    sel_block(i, True)
    o_sel = finish()

    reset()

    def win_body(j, carry):
        kpos = j * tk + krow
        mk = (kpos <= qpos) & (kpos > qpos - WINDOW)
        attend(qr_h, kw_ref[pl.ds(pl.multiple_of(j * tk, tk), tk), :], vwt_ref[j], (mk, mk))
        return carry

    lax.fori_loop(jnp.maximum(i - WINDOW // tk, 0), i + 1, win_body, 0)
    o_win = finish()

    gt = _sigmoid(gate_ref[...]).T
    for r in range(NSA_GROUP):
        def gate(branch):
            c = branch * 2 * NSA_GROUP + 2 * r
            return by_row_half(gt[c:c + 1, :], gt[c + 1:c + 2, :])
        o = gate(0) * occ_ref[r] + gate(1) * o_sel[r] + gate(2) * o_win[r]
        o_refs[r][...] = o.T.astype(o_refs[r].dtype)


def _nsa_attn(qn, qr, kc_bf, vct_bf, ks_bf, vst3, kw_bf, vwt3, gates, *, batch, seq):
    m = qn[0].shape[0]
    tq = KEY_BLOCK
    nq = seq // tq
    n_cmp = seq // CMP_BLOCK
    n_sel = seq // SEL_BLOCK
    qspec = pl.BlockSpec((tq, LANES), lambda b, p, i: (b * nq + i, p))
    kspec = pl.BlockSpec((seq, LANES), lambda b, p, i: (b, p))
    vspec = pl.BlockSpec((nq, LANES, tq), lambda b, p, i: (b, p, 0))
    return pl.pallas_call(
        _nsa_attn_kernel,
        grid=(batch, KV_WIDTH // LANES, nq),
        in_specs=[qspec] * 6 + [pl.BlockSpec((n_cmp, LANES), lambda b, p, i: (b, p)),
                                pl.BlockSpec((LANES, n_cmp), lambda b, p, i: (p, b)),
                                kspec, vspec, kspec, vspec, qspec],
        out_specs=[qspec] * 3,
        out_shape=[jax.ShapeDtypeStruct((m, KV_WIDTH), BF)] * 3,
        scratch_shapes=[pltpu.VMEM((tq // LANES, n_cmp, LANES), F32), pltpu.VMEM((n_sel, tq), F32),
                        pltpu.VMEM((2, n_sel, tq), F32), pltpu.VMEM((8, tq), F32), pltpu.VMEM((8, tq), F32),
                        pltpu.VMEM((NSA_GROUP, LANES, tq), F32), pltpu.VMEM((NSA_GROUP, LANES, tq), F32)],
        compiler_params=_cp("parallel", "parallel", "arbitrary"),
        name="nsa_attn",
    )(*qn, *qr, kc_bf, vct_bf, ks_bf, vst3, kw_bf, vwt3, gates)


def _tile_rows(x, reps):
    return jnp.concatenate([x] * reps, axis=0)


def _block_diag_q(q, n_heads):
    t = q.shape[0]
    assert n_heads * t <= LANES and q.shape[1] == n_heads * HEAD_DIM
    rows = _tile_rows(q, LANES // t)
    rh = jnp.right_shift(lax.broadcasted_iota(jnp.int32, rows.shape, 0), t.bit_length() - 1)
    lh = _head_of(lax.broadcasted_iota(jnp.int32, rows.shape, 1))
    return jnp.where(rh == lh, rows, 0.0).astype(BF)


def _gather_diag(o, n_heads, t):
    lh = _head_of(lax.broadcasted_iota(jnp.int32, (t, o.shape[1]), 1))
    out = jnp.zeros((t, o.shape[1]), F32)
    for h in range(n_heads):
        out = jnp.where(lh == h, o[h * t:(h + 1) * t, :], out)
    return out


def _flat_t(ref):
    x = ref[...]
    return x.reshape(x.shape[0] * x.shape[1], x.shape[2]).astype(BF)


def _pad_rows(pad_ref, x):
    pad_ref[...] = jnp.zeros_like(pad_ref)
    pad_ref[0:x.shape[0], :] = x
    return pad_ref[...].astype(BF)


def _row_query(shape, t):
    return jnp.bitwise_and(lax.broadcasted_iota(jnp.int32, shape, 0), t - 1)


def _softmax_lanes(s):
    e = jnp.exp2(s - jnp.max(s, axis=1, keepdims=True))
    return e / jnp.sum(e, axis=1, keepdims=True)


def _sb_dec_kernel(pps, n_steps, pt_ref, q_ref, kn_ref, vn_ref, *rest):
    kt_refs = rest[:pps]
    vt_refs = rest[pps:2 * pps]
    ut_ref, o_ref, qbd_ref, acc_ref, carry_ref, padk_ref, padv_ref = rest[2 * pps:]
    del pt_ref
    s = pl.program_id(1)
    t = q_ref.shape[0]
    ut = ut_ref[...]

    def block(y, pv, new):
        lk = _sb_log_keep(y)
        if new:
            mask = lax.broadcasted_iota(jnp.int32, y.shape, 1) < _row_query(y.shape, t)
            lk = jnp.where(mask, lk, 0.0)
        after = _dot(lk.astype(BF), ut)
        c = carry_ref[...]
        a = jnp.exp2((lk - y) + after + c)
        if new:
            a = jnp.where(mask, a, 0.0)
        carry_ref[...] = c + jnp.broadcast_to(after[:, 0:1] + lk[:, 0:1], c.shape)
        acc_ref[...] += pv(a.astype(BF))

    @pl.when(s == 0)
    def _():
        qbd_ref[...] = _block_diag_q(q_ref[...], N_SELF_HEADS)
        acc_ref[...] = jnp.zeros_like(acc_ref)
        carry_ref[...] = jnp.zeros_like(carry_ref)
        kn = _pad_rows(padk_ref, kn_ref[...])
        vn = _pad_rows(padv_ref, vn_ref[...])
        block(_dot_nt(qbd_ref[...], kn), lambda a: _dot(a, vn), True)

    for j in range(pps):
        vt = _flat_t(vt_refs[j])
        block(_dot(qbd_ref[...], _flat_t(kt_refs[j])), lambda a, vt=vt: _dot_nt(a, vt), False)

    @pl.when(s == n_steps - 1)
    def _():
        o_ref[...] = _gather_diag(acc_ref[...], N_SELF_HEADS, t)


def _sb_decode(pt, q, k_new, v_new, pool_kt, pool_vt, ut_bf, *, layer, n_pages, pps):
    m = q.shape[0]
    t = m // (pt.shape[0] // n_pages)
    db = m // t
    n_steps = n_pages // pps
    row = pl.BlockSpec((t, SELF_WIDTH), lambda b, s, pt: (b, 0))

    def page_spec(j):
        return pl.BlockSpec((None, None, N_SELF_HEADS, HEAD_DIM, PAGE_SIZE),
                            lambda b, s, pt: (layer, pt[b * n_pages + n_pages - 1 - (s * pps + j)], 0, 0, 0))

    pages = [page_spec(j) for j in range(pps)]
    return pl.pallas_call(
        functools.partial(_sb_dec_kernel, pps, n_steps),
        grid_spec=pltpu.PrefetchScalarGridSpec(
            num_scalar_prefetch=1,
            grid=(db, n_steps),
            in_specs=[row, row, row] + pages + pages + [pl.BlockSpec((LANES, LANES), lambda b, s, pt: (0, 0))],
            out_specs=row,
            scratch_shapes=[pltpu.VMEM((LANES, SELF_WIDTH), BF), pltpu.VMEM((LANES, SELF_WIDTH), F32),
                            pltpu.VMEM((LANES, LANES), F32), pltpu.VMEM((LANES, SELF_WIDTH), F32),
                            pltpu.VMEM((LANES, SELF_WIDTH), F32)]),
        out_shape=jax.ShapeDtypeStruct((m, SELF_WIDTH), F32),
        compiler_params=_cp("parallel", "arbitrary"),
        name="sb_decode",
    )(pt, q, k_new, v_new, *([pool_kt] * pps), *([pool_vt] * pps), ut_bf)


def _mem_dec_kernel(q_ref, kt_ref, vt_ref, o_ref):
    t = q_ref.shape[0]
    qbd = _block_diag_q(q_ref[...], N_MEM_HEADS)
    p = _softmax_lanes(_dot(qbd, _flat_t(kt_ref)))
    o_ref[...] = _gather_diag(_dot_nt(p.astype(BF), _flat_t(vt_ref)), N_MEM_HEADS, t)


def _mem_decode(qm, cache_kt, cache_vt, *, layer, t):
    m = qm.shape[0]
    row = pl.BlockSpec((t, MEM_WIDTH), lambda b: (b, 0))
    cache = pl.BlockSpec((None, None, N_MEM_HEADS, HEAD_DIM, N_MEM), lambda b: (layer, b, 0, 0, 0))
    return pl.pallas_call(
        _mem_dec_kernel,
        grid=(m // t,),
        in_specs=[row, cache, cache],
        out_specs=row,
        out_shape=jax.ShapeDtypeStruct((m, MEM_WIDTH), F32),
        compiler_params=_cp("parallel"),
        name="mem_decode",
    )(qm, cache_kt, cache_vt)


def _cmp_dec_kernel(n_pages, pt_ref, *refs):
    kp = refs[:n_pages]
    vp = refs[n_pages:2 * n_pages]
    pek_ref, pev_ref, wk_ref, wv_ref, ok_ref, ov_ref, xa_ref, xb_ref = refs[2 * n_pages:]
    del pt_ref
    rows = kp[0].shape[2]
    for pages, pe_ref, w_ref, o_ref in ((kp, pek_ref, wk_ref, ok_ref), (vp, pev_ref, wv_ref, ov_ref)):
        for p in range(n_pages):
            x = pages[p][...]
            xt = x.reshape(KV_WIDTH, rows).T
            xa_ref[p * rows:(p + 1) * rows, :] = xt[:, :LANES]
            xb_ref[p * rows:(p + 1) * rows, :] = xt[:, LANES:]
        o_ref[...] = _compress_rows(xa_ref, xb_ref, pe_ref, w_ref, o_ref.shape[0]).astype(o_ref.dtype)


def _cmp_decode(pt, pool_kt, pool_vt, pek2, pev2, wk2, wv2, *, layer, n_pages, db):
    n_cmp = n_pages * PAGE_SIZE // CMP_BLOCK

    def page_spec(p):
        return pl.BlockSpec((None, None, NSA_KV_HEADS, HEAD_DIM, PAGE_SIZE),
                            lambda b, pt: (layer, pt[b * n_pages + p], 0, 0, 0))

    pages = [page_spec(p) for p in range(n_pages)]
    const2 = lambda b, pt: (0, 0)
    const3 = lambda b, pt: (0, 0, 0)
    out = pl.BlockSpec((None, n_cmp, KV_WIDTH), lambda b, pt: (b, 0, 0))
    return pl.pallas_call(
        functools.partial(_cmp_dec_kernel, n_pages),
        grid_spec=pltpu.PrefetchScalarGridSpec(
            num_scalar_prefetch=1,
            grid=(db,),
            in_specs=pages + pages + [pl.BlockSpec(pek2.shape, const2), pl.BlockSpec(pev2.shape, const2),
                                      pl.BlockSpec(wk2.shape, const3), pl.BlockSpec(wv2.shape, const3)],
            out_specs=[out, out],
            scratch_shapes=[pltpu.VMEM((n_pages * PAGE_SIZE, LANES), F32)] * 2),
        out_shape=[jax.ShapeDtypeStruct((db, n_cmp, KV_WIDTH), BF)] * 2,
        compiler_params=_cp("parallel"),
        name="cmp_decode",
    )(pt, *([pool_kt] * n_pages), *([pool_vt] * n_pages), pek2, pev2, wk2, wv2)


def _nsa_dec_kernel(n_pages, past, pt_ref, qn0, qn1, qn2, qr0, qr1, qr2, gate_ref, kc_ref, vc_ref,
                    ksn_ref, vsn_ref, kwn_ref, vwn_ref, wkt_ref, wvt_ref, pm_ref, *rest):
    kp = rest[:n_pages]
    vp = rest[n_pages:2 * n_pages]
    o0, o1, o2, score_ref, ssel_ref, pada_ref, padb_ref = rest[2 * n_pages:]
    del pt_ref
    t = qn0.shape[0]
    t_bits = t.bit_length() - 1
    grp = NSA_KV_HEADS * t

    def build(q_refs):
        parts = []
        for r in range(NSA_GROUP):
            rows = _tile_rows(q_refs[r][...], NSA_KV_HEADS)
            rg = jnp.right_shift(lax.broadcasted_iota(jnp.int32, rows.shape, 0), t_bits)
            lg = _head_of(lax.broadcasted_iota(jnp.int32, rows.shape, 1))
            parts.append(jnp.where(rg == lg, rows, 0.0))
        parts.append(jnp.zeros((LANES - N_SELF_HEADS * t, KV_WIDTH), F32))
        return jnp.concatenate(parts, axis=0).astype(BF)

    qbn = build((qn0, qn1, qn2))
    qbr = build((qr0, qr1, qr2))

    def qpos_of(shape):
        return past + _row_query(shape, t)

    kc = kc_ref[...]
    n_cmp = kc.shape[0]
    shape_c = (LANES, n_cmp)
    c_mask = (lax.broadcasted_iota(jnp.int32, shape_c, 1) * CMP_BLOCK + (CMP_BLOCK - 1)) <= qpos_of(shape_c)
    p = _softmax_lanes(jnp.where(c_mask, _dot_nt(qbn, kc), NEG))
    live = lax.broadcasted_iota(jnp.int32, shape_c, 0) < N_SELF_HEADS * t
    p = jnp.where(c_mask & live, p, 0.0)
    o_cmp = _dot(p.astype(BF), vc_ref[...])

    imp = p[0:grp] + p[grp:2 * grp] + p[2 * grp:3 * grp]
    imp = _tile_rows(imp, LANES // grp)
    hi = imp.astype(BF)
    mid = (imp - hi.astype(F32)).astype(BF)
    lo = (imp - hi.astype(F32) - mid.astype(F32)).astype(BF)
    pm = pm_ref[...]
    blk_imp = _dot(hi, pm) + _dot(mid, pm) + _dot(lo, pm)
    blk = lax.broadcasted_iota(jnp.int32, (LANES, LANES), 1)
    cur_r = _head_of(qpos_of((LANES, LANES)))
    forced = (blk == 0) | (blk == cur_r) | (blk == cur_r - 1)
    score = jnp.where(forced, jnp.inf, jnp.where(blk <= cur_r, blk_imp, -jnp.inf))
    n_pad = score_ref.shape[0]
    score_ref[...] = score.T[0:n_pad, :]
    cur_l = _head_of(past + jnp.bitwise_and(lax.broadcasted_iota(jnp.int32, (1, LANES), 1), t - 1))
    sel_t = _topk_select(score_ref, n_pad, cur_l)
    sel = jnp.concatenate([sel_t, jnp.zeros((LANES - n_pad, LANES), F32)], axis=0).T

    kcol = lax.broadcasted_iota(jnp.int32, (LANES, LANES), 1)
    tq = _row_query((LANES, LANES), t)
    new_ok = (kcol <= tq) & (kcol < t)
    per_page = PAGE_SIZE // SEL_BLOCK

    def chosen(b0):
        cols = [jnp.broadcast_to(sel[:, b0 + u:b0 + u + 1], (LANES, SEL_BLOCK)) for u in range(per_page)]
        return jnp.concatenate(cols, axis=1) > 0.5

    for pg in range(n_pages):
        sc = _dot(qbr, _flat_t(kp[pg]))
        ssel_ref[:, pg * LANES:(pg + 1) * LANES] = jnp.where(chosen(pg * per_page), sc, NEG)
    new_blk = n_pages * per_page
    sc = _dot_nt(qbr, _pad_rows(pada_ref, ksn_ref[...]))
    ok = (jnp.broadcast_to(sel[:, new_blk:new_blk + 1], (LANES, LANES)) > 0.5) & new_ok
    ssel_ref[:, n_pages * LANES:(n_pages + 1) * LANES] = jnp.where(ok, sc, NEG)
    ssel_ref[...] = _softmax_lanes(ssel_ref[...])
    o_sel = _dot(ssel_ref[:, n_pages * LANES:(n_pages + 1) * LANES].astype(BF), _pad_rows(padb_ref, vsn_ref[...]))
    for pg in range(n_pages):
        o_sel = o_sel + _dot_nt(ssel_ref[:, pg * LANES:(pg + 1) * LANES].astype(BF), _flat_t(vp[pg]))

    n_win = wkt_ref.shape[2]
    shape_w = (LANES, n_win)
    wpos = (past - n_win) + lax.broadcasted_iota(jnp.int32, shape_w, 1)
    qpos_w = qpos_of(shape_w)
    s_old = jnp.where((wpos <= qpos_w) & (wpos > qpos_w - WINDOW), _dot(qbr, _flat_t(wkt_ref)), NEG)
    s_new = jnp.where(new_ok, _dot_nt(qbr, _pad_rows(pada_ref, kwn_ref[...])), NEG)
    pw = _softmax_lanes(jnp.concatenate([s_old, s_new], axis=1))
    o_win = (_dot_nt(pw[:, :n_win].astype(BF), _flat_t(wvt_ref))
             + _dot(pw[:, n_win:].astype(BF), _pad_rows(padb_ref, vwn_ref[...])))

    gates = _tile_rows(_sigmoid(gate_ref[...]), LANES // t)
    row = lax.broadcasted_iota(jnp.int32, gates.shape, 0)
    col = lax.broadcasted_iota(jnp.int32, gates.shape, 1)
    r_of = jnp.right_shift(row, t_bits + 2)
    g_of = jnp.bitwise_and(jnp.right_shift(row, t_bits), NSA_KV_HEADS - 1)
    base = jnp.right_shift(g_of, 1) * LANES + r_of * 2 + jnp.bitwise_and(g_of, 1)

    def gate(branch):
        return jnp.sum(jnp.where(col == base + branch * 2 * NSA_GROUP, gates, 0.0), axis=1, keepdims=True)

    o = gate(0) * o_cmp + gate(1) * o_sel + gate(2) * o_win
    for r, o_ref in enumerate((o0, o1, o2)):
        o_ref[...] = _gather_diag(o[r * grp:(r + 1) * grp, :], NSA_KV_HEADS, t)


def _nsa_decode(pt, qn, qr, gates, kc, vc, new_rows, win_kt, win_vt, pool_kt, pool_vt, pair_mat, *, layer, n_pages, t):
    m = qn[0].shape[0]
    db = m // t
    past = n_pages * PAGE_SIZE
    n_cmp = kc.shape[1]
    n_win = win_kt.shape[4]
    n_sel_pad = -(-(-(-(past + t) // SEL_BLOCK)) // 8) * 8
    row = pl.BlockSpec((t, KV_WIDTH), lambda b, pt: (b, 0))
    cmp_spec = pl.BlockSpec((None, n_cmp, KV_WIDTH), lambda b, pt: (b, 0, 0))
    win_spec = pl.BlockSpec((None, None, NSA_KV_HEADS, HEAD_DIM, n_win), lambda b, pt: (layer, b, 0, 0, 0))

    def page_spec(p):
        return pl.BlockSpec((None, None, NSA_KV_HEADS, HEAD_DIM, PAGE_SIZE),
                            lambda b, pt: (layer, pt[b * n_pages + p], 0, 0, 0))

    pages = [page_spec(p) for p in range(n_pages)]
    return pl.pallas_call(
        functools.partial(_nsa_dec_kernel, n_pages, past),
        grid_spec=pltpu.PrefetchScalarGridSpec(
            num_scalar_prefetch=1,
            grid=(db,),
            in_specs=[row] * 7 + [cmp_spec, cmp_spec] + [row] * 4 + [win_spec, win_spec]
            + [pl.BlockSpec(pair_mat.shape, lambda b, pt: (0, 0))] + pages + pages,
            out_specs=[row] * 3,
            scratch_shapes=[pltpu.VMEM((n_sel_pad, LANES), F32), pltpu.VMEM((LANES, (n_pages + 1) * LANES), F32),
                            pltpu.VMEM((LANES, KV_WIDTH), F32), pltpu.VMEM((LANES, KV_WIDTH), F32)]),
        out_shape=[jax.ShapeDtypeStruct((m, KV_WIDTH), F32)] * 3,
        compiler_params=_cp("parallel"),
        name="nsa_decode",
    )(pt, *qn, *qr, gates, kc, vc, *new_rows, win_kt, win_vt, pair_mat,
      *([pool_kt] * n_pages), *([pool_vt] * n_pages))


NSA_IN = SELF_WIDTH + 6 * KV_WIDTH + 3 * N_SELF_HEADS + MEM_WIDTH
ROW_TILE = 512
SB_PAGES_PER_STEP = 4


def _nsa_head_order():
    return [NSA_GROUP * g + r for r in range(NSA_GROUP) for g in range(NSA_KV_HEADS)]


def _nsa_in_columns():
    q = [h * HEAD_DIM + d for h in _nsa_head_order() for d in range(HEAD_DIM)]
    kv = list(range(SELF_WIDTH, SELF_WIDTH + 6 * KV_WIDTH))
    g0 = SELF_WIDTH + 6 * KV_WIDTH
    gate = []
    for gp in range(KV_WIDTH // LANES):
        for c in range(LANES):
            if c < 3 * 2 * NSA_GROUP:
                branch, r, hf = c // (2 * NSA_GROUP), (c % (2 * NSA_GROUP)) // 2, c % 2
                gate.append(g0 + (NSA_GROUP * (2 * gp + hf) + r) * 3 + branch)
            else:
                gate.append(NSA_IN)
    qm = list(range(g0 + 3 * N_SELF_HEADS, NSA_IN))
    cols = np.asarray(q + kv + gate + qm, np.int32)
    assert cols.shape[0] == _NSA_COLS
    return cols


def _nsa_out_rows():
    o = [h * HEAD_DIM + d for h in _nsa_head_order() for d in range(HEAD_DIM)]
    return np.asarray(o + list(range(SELF_WIDTH, SELF_WIDTH + MEM_WIDTH)), np.int32)


def _block_diag2(w):
    eye = jnp.eye(LANES // HEAD_DIM, dtype=w.dtype)
    return jnp.einsum("gh,lde->lgdhe", eye, w).reshape(w.shape[0], LANES, LANES).astype(BF)


def kernel(x_prompt, x_sample, mem_prompt, cache_sb_k, cache_sb_v, cache_nsa_cmp_k, cache_nsa_cmp_v, cache_nsa_sel_k, cache_nsa_sel_v, cache_nsa_win_k, cache_nsa_win_v, cache_mem_k, cache_mem_v, page_table, ln_mix_pre, ln_mix_post, ln_ffn_pre, ln_ffn_post, ln_mem, w_in_a, w_in_b, w_cmp_k, w_cmp_v, pe_cmp_k, pe_cmp_v, w_out, w_mem_kv, w_gate_up, w_down):
    bsz, seq, d = x_prompt.shape
    db, ds, _ = x_sample.shape
    n_pages = page_table.shape[1]
    past = n_pages * PAGE_SIZE
    depth = w_out.shape[0]
    n_win = cache_nsa_win_k.shape[2]
    tm_p = min(ROW_TILE, bsz * seq)
    tm_s = min(ROW_TILE, db * ds)
    assert d == D_MODEL and seq % KEY_BLOCK == 0 and seq % tm_p == 0 and (db * ds) % tm_s == 0 and tm_s % ds == 0
    assert ds & (ds - 1) == 0 and N_SELF_HEADS * ds <= LANES and ds < CMP_BLOCK
    assert past % SEL_BLOCK == 0 and past + ds <= past + SEL_BLOCK and n_win % LANES == 0 and n_win <= past
    assert n_pages % SB_PAGES_PER_STEP == 0 and seq >= WINDOW

    h_p = x_prompt.reshape(bsz * seq, d)
    h_s = x_sample.reshape(db * ds, d)
    mem = mem_prompt.reshape(bsz * N_MEM, d)
    pt = page_table.reshape(-1).astype(jnp.int32)
    u256 = jnp.triu(jnp.ones((KEY_BLOCK, KEY_BLOCK), BF), 1)
    ut128 = jnp.tril(jnp.ones((LANES, LANES), BF), -1)
    pair_np = np.zeros((past // CMP_BLOCK, LANES), np.float32)
    pair_np[np.arange(past // CMP_BLOCK), np.arange(past // CMP_BLOCK) // (SEL_BLOCK // CMP_BLOCK)] = 1.0
    pair_mat = jnp.asarray(pair_np, BF)
    tmin = lambda c: jnp.transpose(c, (0, 1, 3, 4, 2))
    sbk_t, sbv_t = tmin(cache_sb_k), tmin(cache_sb_v)
    cmpk_t, cmpv_t = tmin(cache_nsa_cmp_k), tmin(cache_nsa_cmp_v)
    selk_t, selv_t = tmin(cache_nsa_sel_k), tmin(cache_nsa_sel_v)
    wink_t, winv_t = tmin(cache_nsa_win_k), tmin(cache_nsa_win_v)
    memk_t, memv_t = tmin(cache_mem_k), tmin(cache_mem_v)
    rope_p = _rope_tables(jnp.arange(seq, dtype=jnp.int32))
    rope_s = _rope_tables(past + jnp.arange(tm_s, dtype=jnp.int32) % ds)

    nsa_cols = _nsa_in_columns()
    nsa_rows = _nsa_out_rows()

    sb_p, sb_s = [[], []], [[], []]
    nsa_p, nsa_s = [[] for _ in range(6)], [[] for _ in range(6)]
    memk_p, memv_p = [], []
    for i in range(depth):
        j = i // 2
        g_pre = ln_mix_pre[i][None]
        tail_w = (ln_mix_post[i][None], ln_ffn_pre[i][None], ln_ffn_post[i][None],
                  w_gate_up[i].astype(BF), w_down[i].astype(BF))
        wkv = w_mem_kv[i]
        mk, mv, mk_bf, mvt_bf = _memkv(mem, ln_mem[i][None], wkv.astype(BF), wkv[:, MEM_WIDTH:].T.astype(BF))
        memk_p.append(mk.reshape(bsz, N_MEM, N_MEM_HEADS, HEAD_DIM))
        memv_p.append(mv.reshape(bsz, N_MEM, N_MEM_HEADS, HEAD_DIM))
        if i % 2 == 0:
            w = w_in_a[j]
            w_bf = w.astype(BF)
            wvt = w[:, 2 * SELF_WIDTH:3 * SELF_WIDTH].T.astype(BF)
            q, k, v, kb, qm, vt3 = _sb_inproj(h_p, g_pre, w_bf, wvt, tm=tm_p, with_vt=True, q_dtype=BF)
            o = _sb_attn(q, kb, vt3, u256, batch=bsz, seq=seq)
            om = _mem_attn(qm, mk_bf, mvt_bf, batch=bsz, seq=seq)
            qs, ks, vs, _, qms = _sb_inproj(h_s, g_pre, w_bf, wvt, tm=tm_s, with_vt=False, q_dtype=F32)
            o_s = _sb_decode(pt, qs, ks, vs, sbk_t, sbv_t, ut128, layer=j, n_pages=n_pages,
                             pps=SB_PAGES_PER_STEP)
            om_s = _mem_decode(qms, memk_t, memv_t, layer=i, t=ds)
            wo = w_out[i].astype(BF)
            mix_p, mix_s = [o, om], [o_s, om_s]
            for lst, a, shape in ((sb_p, (k, v), (bsz, seq)), (sb_s, (ks, vs), (db, ds))):
                for n in range(2):
                    lst[n].append(a[n].reshape(*shape, N_SELF_HEADS, HEAD_DIM))
        else:
            w = w_in_b[j]
            w_bf = jnp.concatenate([w, jnp.zeros((d, 1), w.dtype)], axis=1)[:, nsa_cols].astype(BF)
            vs0 = SELF_WIDTH + 3 * KV_WIDTH
            vw0 = SELF_WIDTH + 5 * KV_WIDTH
            wt = jnp.concatenate([w[:, vs0:vs0 + KV_WIDTH], w[:, vw0:vw0 + KV_WIDTH]], axis=1).T.astype(BF)
            pek2 = jnp.tile(pe_cmp_k[j], (1, LANES // HEAD_DIM))
            pev2 = jnp.tile(pe_cmp_v[j], (1, LANES // HEAD_DIM))
            wck2 = _block_diag2(w_cmp_k[j])
            wcv2 = _block_diag2(w_cmp_v[j])
            outs = _nsa_inproj(h_p, g_pre, w_bf, wt, *rope_p, tm=tm_p, with_vt=True, q_dtype=BF)
            qn, qr, rows_p = outs[0:3], outs[3:6], outs[6:12]
            ksb, kwb, gates, qm, vst3, vwt3 = outs[12:18]
            kc_bf, _ = _compress(rows_p[0], pek2, wck2, rows=seq)
            _, vct_bf = _compress(rows_p[1], pev2, wcv2, rows=seq)
            o3 = _nsa_attn(qn, qr, kc_bf, vct_bf, ksb, vst3, kwb, vwt3, gates, batch=bsz, seq=seq)
            om = _mem_attn(qm, mk_bf, mvt_bf, batch=bsz, seq=seq)
            outs = _nsa_inproj(h_s, g_pre, w_bf, wt, *rope_s, tm=tm_s, with_vt=False, q_dtype=F32)
            qn_s, qr_s, rows_s = outs[0:3], outs[3:6], outs[6:12]
            gates_s, qms = outs[14], outs[15]
            kc_s, vc_s = _cmp_decode(pt, cmpk_t, cmpv_t, pek2, pev2, wck2, wcv2, layer=j, n_pages=n_pages, db=db)
            o3_s = _nsa_decode(pt, qn_s, qr_s, gates_s, kc_s, vc_s, rows_s[2:6], wink_t, winv_t, selk_t, selv_t,
                               pair_mat, layer=j, n_pages=n_pages, t=ds)
            om_s = _mem_decode(qms, memk_t, memv_t, layer=i, t=ds)
            wo = w_out[i][nsa_rows].astype(BF)
            mix_p, mix_s = [*o3, om], [*o3_s, om_s]
            for n in range(6):
                rp = rows_p[n].reshape(bsz, seq, NSA_KV_HEADS, HEAD_DIM)
                nsa_p[n].append(rp[:, seq - min(WINDOW, seq):] if n >= 4 else rp)
                nsa_s[n].append(rows_s[n].reshape(db, ds, NSA_KV_HEADS, HEAD_DIM))
        h_p = _layer_tail(h_p, mix_p, wo, *tail_w, tm=tm_p)
        h_s = _layer_tail(h_s, mix_s, wo, *tail_w, tm=tm_s)

    st = lambda rows: jnp.stack(rows, axis=0)
    return (h_p.reshape(bsz, seq, d), h_s.reshape(db, ds, d),
            st(sb_p[0]), st(sb_p[1]),
            *[st(x) for x in nsa_p],
            st(memk_p), st(memv_p),
            st(sb_s[0]), st(sb_s[1]),
            *[st(x) for x in nsa_s])
```

```python
import functools

import jax
import jax.numpy as jnp
import numpy as np
from jax import lax
from jax.experimental import pallas as pl
from jax.experimental.pallas import tpu as pltpu

F32 = jnp.float32
BF = jnp.bfloat16

D_MODEL = 1024
HEAD_DIM = 64
N_SELF_HEADS = 12
N_MEM_HEADS = 4
SELF_WIDTH = N_SELF_HEADS * HEAD_DIM
MEM_WIDTH = N_MEM_HEADS * HEAD_DIM
N_MEM = 256
NSA_KV_HEADS = 4
NSA_GROUP = N_SELF_HEADS // NSA_KV_HEADS
KV_WIDTH = NSA_KV_HEADS * HEAD_DIM
CMP_BLOCK = 32
SEL_BLOCK = 64
SEL_TOPK = 16
WINDOW = 512
PAGE_SIZE = 128
ROPE_THETA = 500000.0
ROT_DIM = HEAD_DIM // 4
EPS = 1e-6
NEG = -1e30
SCALE = HEAD_DIM ** -0.5
Q_SCALE = SCALE * 1.4426950408889634

LANES = 128
KEY_BLOCK = 256
VMEM_LIMIT = 56 * 1024 * 1024


def _cp(*sem):
    return pltpu.CompilerParams(dimension_semantics=sem, vmem_limit_bytes=VMEM_LIMIT)


def _dot(a, b):
    return jnp.dot(a, b, preferred_element_type=F32)


def _dot_nt(a, b):
    return lax.dot_general(a, b, (((1,), (1,)), ((), ())), preferred_element_type=F32)


def _dot_tn(a, b):
    return lax.dot_general(a, b, (((0,), (0,)), ((), ())), preferred_element_type=F32)


def _rms(x, g):
    return x * lax.rsqrt(jnp.mean(x * x, axis=-1, keepdims=True) + EPS) * g


def _sigmoid(x):
    return 1.0 / (1.0 + jnp.exp(-x))


def _neg_abs(y):
    bits = lax.bitcast_convert_type(y, jnp.int32) | jnp.int32(-2 ** 31)
    return lax.bitcast_convert_type(bits, F32)


def _sb_log_keep(y):
    return jnp.minimum(y, 0.0) - jnp.log2(1.0 + jnp.exp2(_neg_abs(y)))


def _head_of(idx):
    return jnp.right_shift(idx, 6)


def _const_spec(shape):
    nd = len(shape)
    return pl.BlockSpec(shape, lambda *_: (0,) * nd, pipeline_mode=pl.Buffered(1))


def _rope128(x, c, s):
    lane = lax.broadcasted_iota(jnp.int32, x.shape, 1)
    up = pltpu.roll(x, LANES - ROT_DIM // 2, axis=1)
    dn = pltpu.roll(x, ROT_DIM // 2, axis=1)
    sw = jnp.where(jnp.bitwise_and(lane, HEAD_DIM - 1) < ROT_DIM // 2, up, dn)
    return x * c + sw * s


def _rope_tables(pos):
    half = ROT_DIM // 2
    inv = ROPE_THETA ** (-jnp.arange(half, dtype=F32) * 2.0 / ROT_DIM)
    ang = pos.astype(F32)[:, None] * inv[None, :]
    cos, sin = jnp.cos(ang), jnp.sin(ang)
    n = pos.shape[0]
    c64 = jnp.concatenate([cos, cos, jnp.ones((n, HEAD_DIM - ROT_DIM), F32)], axis=1)
    s64 = jnp.concatenate([-sin, sin, jnp.zeros((n, HEAD_DIM - ROT_DIM), F32)], axis=1)
    return jnp.tile(c64, (1, LANES // HEAD_DIM)), jnp.tile(s64, (1, LANES // HEAD_DIM))


def _sb_inproj_kernel(x_ref, g_ref, w_ref, wvt_ref, q_ref, k_ref, v_ref, kb_ref, qm_ref, *vt_refs):
    xn = _rms(x_ref[...], g_ref[...]).astype(BF)
    q_ref[...] = (_dot(xn, w_ref[:, 0:SELF_WIDTH]) * -Q_SCALE).astype(q_ref.dtype)
    k = _dot(xn, w_ref[:, SELF_WIDTH:2 * SELF_WIDTH])
    k_ref[...] = k
    kb_ref[...] = k.astype(BF)
    v_ref[...] = _dot(xn, w_ref[:, 2 * SELF_WIDTH:3 * SELF_WIDTH])
    qm_ref[...] = (_dot(xn, w_ref[:, 3 * SELF_WIDTH:3 * SELF_WIDTH + MEM_WIDTH]) * Q_SCALE).astype(qm_ref.dtype)
    if vt_refs:
        (vt_ref,) = vt_refs
        vt = _dot_nt(wvt_ref[...], xn)
        for c in range(vt_ref.shape[0]):
            vt_ref[c] = vt[:, c * KEY_BLOCK:(c + 1) * KEY_BLOCK].astype(BF)


def _sb_inproj(x, g, w_bf, wvt_bf, *, tm, with_vt, q_dtype):
    m = x.shape[0]
    grid = (m // tm,)
    row = lambda i: (i, 0)
    const = lambda i: (0, 0)
    out_shape = [jax.ShapeDtypeStruct((m, SELF_WIDTH), q_dtype),
                 jax.ShapeDtypeStruct((m, SELF_WIDTH), F32),
                 jax.ShapeDtypeStruct((m, SELF_WIDTH), F32),
                 jax.ShapeDtypeStruct((m, SELF_WIDTH), BF),
                 jax.ShapeDtypeStruct((m, MEM_WIDTH), q_dtype)]
    out_specs = [pl.BlockSpec((tm, SELF_WIDTH), row)] * 4 + [pl.BlockSpec((tm, MEM_WIDTH), row)]
    if with_vt:
        out_shape.append(jax.ShapeDtypeStruct((m // KEY_BLOCK, SELF_WIDTH, KEY_BLOCK), BF))
        out_specs.append(pl.BlockSpec((tm // KEY_BLOCK, SELF_WIDTH, KEY_BLOCK), lambda i: (i, 0, 0)))
    return pl.pallas_call(
        _sb_inproj_kernel,
        grid=grid,
        in_specs=[pl.BlockSpec((tm, D_MODEL), row), _const_spec((1, D_MODEL)),
                  _const_spec(w_bf.shape), _const_spec(wvt_bf.shape)],
        out_specs=out_specs,
        out_shape=out_shape,
        compiler_params=_cp("parallel"),
        name="sb_inproj",
    )(x, g, w_bf, wvt_bf)


_NSA_Q0 = 0
_NSA_KV0 = SELF_WIDTH
_NSA_G0 = _NSA_KV0 + 6 * KV_WIDTH
_NSA_QM0 = _NSA_G0 + 2 * LANES
_NSA_COLS = _NSA_QM0 + MEM_WIDTH


def _nsa_inproj_kernel(x_ref, g_ref, w_ref, wt_ref, c_ref, s_ref,
                       qn0, qn1, qn2, qr0, qr1, qr2, kc_ref, vc_ref, ks_ref, vs_ref, kw_ref, vw_ref,
                       ksb_ref, kwb_ref, gate_ref, qm_ref, *vt_refs):
    xn = _rms(x_ref[...], g_ref[...]).astype(BF)
    c = c_ref[...]
    s = s_ref[...]

    def rope256(y):
        return jnp.concatenate([_rope128(y[:, :LANES], c, s), _rope128(y[:, LANES:], c, s)], axis=1)

    for r, (qn_ref, qr_ref) in enumerate(((qn0, qr0), (qn1, qr1), (qn2, qr2))):
        q = _dot(xn, w_ref[:, r * KV_WIDTH:(r + 1) * KV_WIDTH]) * Q_SCALE
        qn_ref[...] = q.astype(qn_ref.dtype)
        qr_ref[...] = rope256(q).astype(qr_ref.dtype)

    def kv(j):
        return _dot(xn, w_ref[:, _NSA_KV0 + j * KV_WIDTH:_NSA_KV0 + (j + 1) * KV_WIDTH])

    kc_ref[...] = kv(0)
    vc_ref[...] = kv(1)
    ks = rope256(kv(2))
    ks_ref[...] = ks
    ksb_ref[...] = ks.astype(BF)
    vs_ref[...] = kv(3)
    kw = rope256(kv(4))
    kw_ref[...] = kw
    kwb_ref[...] = kw.astype(BF)
    vw_ref[...] = kv(5)
    gate_ref[...] = _dot(xn, w_ref[:, _NSA_G0:_NSA_G0 + 2 * LANES])
    qm_ref[...] = (_dot(xn, w_ref[:, _NSA_QM0:_NSA_QM0 + MEM_WIDTH]) * Q_SCALE).astype(qm_ref.dtype)
    if vt_refs:
        vst_ref, vwt_ref = vt_refs
        vt = _dot_nt(wt_ref[...], xn)
        for cblk in range(vst_ref.shape[0]):
            sl = slice(cblk * KEY_BLOCK, (cblk + 1) * KEY_BLOCK)
            vst_ref[cblk] = vt[:KV_WIDTH, sl].astype(BF)
            vwt_ref[cblk] = vt[KV_WIDTH:, sl].astype(BF)


def _nsa_inproj(x, g, w_bf, wt_bf, ctab, stab, *, tm, with_vt, q_dtype):
    m = x.shape[0]
    grid = (m // tm,)
    row = lambda i: (i, 0)
    const = lambda i: (0, 0)
    tbl_blocks = ctab.shape[0] // tm
    tbl = lambda i: (i % tbl_blocks, 0)
    kvs = lambda dt: jax.ShapeDtypeStruct((m, KV_WIDTH), dt)
    out_shape = [kvs(q_dtype)] * 6 + [kvs(F32)] * 6 + [kvs(BF)] * 2 + [kvs(F32), kvs(q_dtype)]
    out_specs = [pl.BlockSpec((tm, KV_WIDTH), row)] * 16
    if with_vt:
        out_shape += [jax.ShapeDtypeStruct((m // KEY_BLOCK, KV_WIDTH, KEY_BLOCK), BF)] * 2
        out_specs += [pl.BlockSpec((tm // KEY_BLOCK, KV_WIDTH, KEY_BLOCK), lambda i: (i, 0, 0))] * 2
    return pl.pallas_call(
        _nsa_inproj_kernel,
        grid=grid,
        in_specs=[pl.BlockSpec((tm, D_MODEL), row), _const_spec((1, D_MODEL)),
                  _const_spec(w_bf.shape), _const_spec(wt_bf.shape),
                  pl.BlockSpec((tm, LANES), tbl), pl.BlockSpec((tm, LANES), tbl)],
        out_specs=out_specs,
        out_shape=out_shape,
        compiler_params=_cp("parallel"),
        name="nsa_inproj",
    )(x, g, w_bf, wt_bf, ctab, stab)


def _memkv_kernel(x_ref, g_ref, w_ref, wvt_ref, k_ref, v_ref, kb_ref, vt_ref):
    xn = _rms(x_ref[...], g_ref[...]).astype(BF)
    k = _dot(xn, w_ref[:, :MEM_WIDTH])
    k_ref[...] = k
    kb_ref[...] = k.astype(BF)
    v_ref[...] = _dot(xn, w_ref[:, MEM_WIDTH:])
    vt_ref[...] = _dot_nt(wvt_ref[...], xn).astype(BF)


def _memkv(x, g, w_bf, wvt_bf):
    m = x.shape[0]
    row = lambda i: (i, 0)
    const = lambda i: (0, 0)
    return pl.pallas_call(
        _memkv_kernel,
        grid=(m // N_MEM,),
        in_specs=[pl.BlockSpec((N_MEM, D_MODEL), row), _const_spec((1, D_MODEL)),
                  _const_spec(w_bf.shape), _const_spec(wvt_bf.shape)],
        out_specs=[pl.BlockSpec((N_MEM, MEM_WIDTH), row)] * 3 + [pl.BlockSpec((MEM_WIDTH, N_MEM), lambda i: (0, i))],
        out_shape=[jax.ShapeDtypeStruct((m, MEM_WIDTH), F32)] * 2 + [jax.ShapeDtypeStruct((m, MEM_WIDTH), BF),
                                                                     jax.ShapeDtypeStruct((MEM_WIDTH, m), BF)],
        compiler_params=_cp("parallel"),
        name="memkv",
    )(x, g, w_bf, wvt_bf)


def _tail_kernel(n_mix, *refs):
    h_ref = refs[0]
    mix_refs = refs[1:1 + n_mix]
    wo_ref, gpost_ref, gpre_ref, gfpost_ref, wgu_ref, wd_ref, o_ref = refs[1 + n_mix:]
    y = None
    off = 0
    for mref in mix_refs:
        w = mref.shape[1]
        part = _dot(mref[...].astype(BF), wo_ref[off:off + w, :])
        y = part if y is None else y + part
        off += w
    h = h_ref[...] + _rms(y, gpost_ref[...])
    xn = _rms(h, gpre_ref[...]).astype(BF)
    d_ff = wd_ref.shape[0]
    acc = None
    for c0 in range(0, d_ff, KEY_BLOCK):
        gt = _dot(xn, wgu_ref[:, c0:c0 + KEY_BLOCK])
        up = _dot(xn, wgu_ref[:, d_ff + c0:d_ff + c0 + KEY_BLOCK])
        act = (gt * _sigmoid(gt) * up).astype(BF)
        part = _dot(act, wd_ref[c0:c0 + KEY_BLOCK, :])
        acc = part if acc is None else acc + part
    o_ref[...] = h + _rms(acc, gfpost_ref[...])


def _layer_tail(h, mixes, wo_bf, g_post, g_pre, g_fpost, wgu_bf, wd_bf, *, tm):
    m = h.shape[0]
    row = lambda i: (i, 0)
    const = lambda i: (0, 0)
    return pl.pallas_call(
        functools.partial(_tail_kernel, len(mixes)),
        grid=(m // tm,),
        in_specs=[pl.BlockSpec((tm, D_MODEL), row)] + [pl.BlockSpec((tm, x.shape[1]), row) for x in mixes]
        + [_const_spec(wo_bf.shape)] + [_const_spec((1, D_MODEL))] * 3
        + [_const_spec(wgu_bf.shape), _const_spec(wd_bf.shape)],
        out_specs=pl.BlockSpec((tm, D_MODEL), row),
        out_shape=jax.ShapeDtypeStruct((m, D_MODEL), F32),
        compiler_params=_cp("parallel"),
        name="layer_tail",
    )(h, *mixes, wo_bf, g_post, g_pre, g_fpost, wgu_bf, wd_bf)


def _half_masks(shape, axis):
    idx = lax.broadcasted_iota(jnp.int32, shape, axis)
    return idx < HEAD_DIM, idx >= HEAD_DIM


def _sb_attn_kernel(q_ref, k_ref, vt_ref, u_ref, o_ref, acc_ref, carry_ref):
    tq = q_ref.shape[0]
    i = pl.program_id(2)
    qi = q_ref[...]
    lo_l, hi_l = _half_masks(qi.shape, 1)
    zero = jnp.zeros_like(qi)
    qh = (jnp.where(lo_l, qi, zero), jnp.where(hi_l, qi, zero))
    acc_ref[...] = jnp.zeros_like(acc_ref)
    carry_ref[...] = jnp.zeros_like(carry_ref)
    u = u_ref[...]

    def run(blocks):
        carry = [carry_ref[h:h + 1, :] for h in range(2)]
        staged = []
        for j, diag in blocks:
            kb = k_ref[pl.ds(pl.multiple_of(j * tq, tq), tq), :]
            vt = vt_ref[j]
            lo_r, hi_r = _half_masks(vt.shape, 0)
            zv = jnp.zeros_like(vt)
            vth = (jnp.where(lo_r, vt, zv), jnp.where(hi_r, vt, zv))
            mask = None
            if diag:
                kr = lax.broadcasted_iota(jnp.int32, (tq, tq), 0)
                qc = lax.broadcasted_iota(jnp.int32, (tq, tq), 1)
                mask = kr < qc
            for h in range(2):
                y = _dot_nt(kb, qh[h])
                lk = _sb_log_keep(y)
                if diag:
                    lk = jnp.where(mask, lk, 0.0)
                after = _dot(u, lk.astype(BF))
                staged.append((h, y, lk, after, vth[h], mask))
        contrib = None
        for h, y, lk, after, vth_h, mask in staged:
            a = jnp.exp2((lk - y) + after + carry[h])
            if mask is not None:
                a = jnp.where(mask, a, 0.0)
            carry[h] = carry[h] + after[0:1, :] + lk[0:1, :]
            part = _dot(vth_h, a.astype(BF))
            contrib = part if contrib is None else contrib + part
        acc_ref[...] += contrib
        for h in range(2):
            carry_ref[h:h + 1, :] = carry[h]

    run([(i, True)])
    odd = jnp.bitwise_and(i, 1)

    @pl.when(odd == 1)
    def _():
        run([(i - 1, False)])

    def body(jj, c):
        j = i - 1 - odd - 2 * jj
        run([(j, False), (j - 1, False)])
        return c

    lax.fori_loop(0, jnp.right_shift(i, 1), body, 0)
    o_ref[...] = acc_ref[...].T.astype(o_ref.dtype)


def _sb_attn(q_bf, k_bf, vt3, u_bf, *, batch, seq):
    m = q_bf.shape[0]
    tq = KEY_BLOCK
    nq = seq // tq
    return pl.pallas_call(
        _sb_attn_kernel,
        grid=(batch, SELF_WIDTH // LANES, nq),
        in_specs=[pl.BlockSpec((tq, LANES), lambda b, p, i: (b * nq + i, p)),
                  pl.BlockSpec((seq, LANES), lambda b, p, i: (b, p)),
                  pl.BlockSpec((nq, LANES, tq), lambda b, p, i: (b, p, 0)),
                  _const_spec((tq, tq))],
        out_specs=pl.BlockSpec((tq, LANES), lambda b, p, i: (b * nq + i, p)),
        out_shape=jax.ShapeDtypeStruct((m, SELF_WIDTH), BF),
        scratch_shapes=[pltpu.VMEM((LANES, tq), F32), pltpu.VMEM((8, tq), F32)],
        compiler_params=_cp("parallel", "parallel", "arbitrary"),
        name="sb_attn",
    )(q_bf, k_bf, vt3, u_bf)


def _mem_attn_kernel(q_ref, k_ref, vt_ref, o_ref):
    qi = q_ref[...]
    k = k_ref[...]
    vt = vt_ref[...]
    lane_h = _head_of(lax.broadcasted_iota(jnp.int32, qi.shape, 1))
    row_h = _head_of(lax.broadcasted_iota(jnp.int32, vt.shape, 0))
    acc = None
    for h in range(N_MEM_HEADS):
        s = _dot_nt(k, jnp.where(lane_h == h, qi, jnp.zeros_like(qi)))
        e = jnp.exp2(s - jnp.max(s, axis=0, keepdims=True))
        p = e / jnp.sum(e, axis=0, keepdims=True)
        part = _dot(jnp.where(row_h == h, vt, jnp.zeros_like(vt)), p.astype(BF))
        acc = part if acc is None else acc + part
    o_ref[...] = acc.T.astype(o_ref.dtype)


def _mem_attn(qm_bf, mk_bf, mvt_bf, *, batch, seq):
    m = qm_bf.shape[0]
    tq = KEY_BLOCK
    nq = seq // tq
    return pl.pallas_call(
        _mem_attn_kernel,
        grid=(batch, nq),
        in_specs=[pl.BlockSpec((tq, MEM_WIDTH), lambda b, i: (b * nq + i, 0)),
                  pl.BlockSpec((N_MEM, MEM_WIDTH), lambda b, i: (b, 0)),
                  pl.BlockSpec((MEM_WIDTH, N_MEM), lambda b, i: (0, b))],
        out_specs=pl.BlockSpec((tq, MEM_WIDTH), lambda b, i: (b * nq + i, 0)),
        out_shape=jax.ShapeDtypeStruct((m, MEM_WIDTH), BF),
        compiler_params=_cp("parallel", "parallel"),
        name="mem_attn",
    )(qm_bf, mk_bf, mvt_bf)


def _compress_rows(xa_ref, xb_ref, pe_ref, w_ref, nblk):
    acc = None
    for l in range(CMP_BLOCK):
        xs = [ref[pl.ds(l, nblk, stride=CMP_BLOCK), :] for ref in (xa_ref, xb_ref)]
        xl = jnp.concatenate(xs, axis=0) + pe_ref[l:l + 1, :]
        part = _dot(xl.astype(BF), w_ref[l])
        acc = part if acc is None else acc + part
    return jnp.concatenate([acc[:nblk], acc[nblk:]], axis=1)


def _compress_kernel(xa_ref, xb_ref, pe_ref, w_ref, o_ref, ot_ref):
    out = _compress_rows(xa_ref, xb_ref, pe_ref, w_ref, o_ref.shape[0])
    o_ref[...] = out.astype(o_ref.dtype)
    ot_ref[...] = out.T.astype(ot_ref.dtype)


def _compress(x, pe2, wbd2, *, rows):
    m = x.shape[0]
    nblk = rows // CMP_BLOCK
    return pl.pallas_call(
        _compress_kernel,
        grid=(m // rows,),
        in_specs=[pl.BlockSpec((rows, LANES), lambda i: (i, 0)), pl.BlockSpec((rows, LANES), lambda i: (i, 1)),
                  _const_spec(pe2.shape), _const_spec(wbd2.shape)],
        out_specs=[pl.BlockSpec((nblk, KV_WIDTH), lambda i: (i, 0)), pl.BlockSpec((KV_WIDTH, nblk), lambda i: (0, i))],
        out_shape=[jax.ShapeDtypeStruct((m // CMP_BLOCK, KV_WIDTH), BF),
                   jax.ShapeDtypeStruct((KV_WIDTH, m // CMP_BLOCK), BF)],
        compiler_params=_cp("parallel"),
        name="compress",
    )(x, x, pe2, wbd2)


def _topk_select(score_ref, n_blocks, cur):
    score = score_ref[...]
    blk = lax.broadcasted_iota(jnp.int32, score.shape, 0)

    def body(i, cnt):
        si = score_ref[pl.ds(i, 1), :]
        tie = jnp.where(blk > i, 1.0, 0.0)
        return cnt + jnp.where(si > score, 1.0, jnp.where(si == score, tie, 0.0))

    rank = lax.fori_loop(0, n_blocks, body, jnp.zeros(score.shape, F32))
    return jnp.where((rank < SEL_TOPK) & (blk <= cur), 1.0, 0.0).astype(F32)


def _nsa_attn_kernel(qn0, qn1, qn2, qr0, qr1, qr2, kc_ref, vct_ref, ks_ref, vst_ref, kw_ref, vwt_ref, gate_ref,
                     o0, o1, o2, imp_ref, score_ref, sel_ref, acc_ref, occ_ref):
    tq = qn0.shape[0]
    tk = tq
    i = pl.program_id(2)
    qn_refs = (qn0, qn1, qn2)
    qr_refs = (qr0, qr1, qr2)
    o_refs = (o0, o1, o2)
    qpos = i * tq + lax.broadcasted_iota(jnp.int32, (1, tq), 1)

    def halves(x):
        lo, hi = _half_masks(x.shape, 1)
        z = jnp.zeros_like(x)
        return jnp.where(lo, x, z), jnp.where(hi, x, z)

    def by_row_half(a, b):
        lo, _ = _half_masks((LANES, tq), 0)
        return jnp.where(lo, a, b)

    def vhalves(vt):
        lo, hi = _half_masks(vt.shape, 0)
        z = jnp.zeros_like(vt)
        return jnp.where(lo, vt, z), jnp.where(hi, vt, z)

    n_cmp = kc_ref.shape[0]
    kc = kc_ref[...]
    vct = vhalves(vct_ref[...])
    nrow = lax.broadcasted_iota(jnp.int32, (n_cmp, tq), 0)
    c_mask = (nrow * CMP_BLOCK + (CMP_BLOCK - 1)) <= qpos
    imp = [None, None]
    for r in range(NSA_GROUP):
        qh = halves(qn_refs[r][...])
        occ = None
        for hf in range(2):
            s = jnp.where(c_mask, _dot_nt(kc, qh[hf]), NEG)
            e = jnp.exp2(s - jnp.max(s, axis=0, keepdims=True))
            p = jnp.where(c_mask, e / jnp.sum(e, axis=0, keepdims=True), 0.0)
            imp[hf] = p if imp[hf] is None else imp[hf] + p
            part = _dot(vct[hf], p.astype(BF))
            occ = part if occ is None else occ + part
        occ_ref[r] = occ

    n_sel = sel_ref.shape[1]
    ratio = SEL_BLOCK // CMP_BLOCK
    blk = lax.broadcasted_iota(jnp.int32, (n_sel, tq), 0)
    cur = _head_of(qpos)
    forced = (blk == 0) | (blk == cur) | (blk == cur - 1)
    allowed = blk <= cur
    scores = []
    for hf in range(2):
        parts = []
        for c in range(tq // LANES):
            imp_ref[c] = imp[hf][:, c * LANES:(c + 1) * LANES]
            parts.append(imp_ref[c, pl.ds(0, n_sel, stride=ratio), :] + imp_ref[c, pl.ds(1, n_sel, stride=ratio), :])
        blk_imp = jnp.concatenate(parts, axis=1)
        scores.append(jnp.where(forced, jnp.inf, jnp.where(allowed, blk_imp, -jnp.inf)))
    for hf in range(2):
        score_ref[...] = scores[hf]
        sel_ref[hf] = _topk_select(score_ref, jnp.minimum(n_sel, (i + 1) * (tq // SEL_BLOCK)), cur)

    qr_h = [halves(qr_refs[r][...]) for r in range(NSA_GROUP)]
    krow = lax.broadcasted_iota(jnp.int32, (tk, tq), 0)
    n_sub = tk // SEL_BLOCK
    n_heads = 2 * NSA_GROUP

    def attend(k_ref, vt_ref, mask_of, j_lo):
        def scores(j, last):
            kb = k_ref[pl.ds(pl.multiple_of(j * tk, tk), tk), :]
            masks = mask_of(j, last)
            return [jnp.where(masks[h % 2], _dot_nt(kb, qr_h[h // 2][h % 2]), NEG) for h in range(n_heads)]

        def max_step(j, last, ms):
            s = scores(j, last)
            return tuple(jnp.maximum(ms[h], jnp.max(s[h], axis=0, keepdims=True)) for h in range(n_heads))

        ms = lax.fori_loop(j_lo, i, lambda j, c: max_step(j, False, c),
                           tuple(jnp.full((1, tq), NEG, F32) for _ in range(n_heads)))
        ms = max_step(i, True, ms)
        acc_ref[...] = jnp.zeros_like(acc_ref)

        def sum_step(j, last, ls):
            s = scores(j, last)
            vth = vhalves(vt_ref[j])
            ls = list(ls)
            for r in range(NSA_GROUP):
                contrib = None
                for hf in range(2):
                    h = 2 * r + hf
                    p = jnp.exp2(s[h] - ms[h])
                    ls[h] = ls[h] + jnp.sum(p, axis=0, keepdims=True)
                    part = _dot(vth[hf], p.astype(BF))
                    contrib = part if contrib is None else contrib + part
                acc_ref[r] += contrib
            return tuple(ls)

        ls = lax.fori_loop(j_lo, i, lambda j, c: sum_step(j, False, c),
                           tuple(jnp.zeros((1, tq), F32) for _ in range(n_heads)))
        ls = sum_step(i, True, ls)
        return [acc_ref[r] * by_row_half(1.0 / ls[2 * r], 1.0 / ls[2 * r + 1]) for r in range(NSA_GROUP)]

    def sel_mask(j, last):
        masks = []
        for hf in range(2):
            selv = jnp.concatenate(
                [jnp.broadcast_to(sel_ref[hf, pl.ds(j * n_sub + u, 1), :], (SEL_BLOCK, tq)) for u in range(n_sub)],
                axis=0)
            mk = selv > 0.5
            if last:
                mk = mk & ((j * tk + krow) <= qpos)
            masks.append(mk)
        return masks

    o_sel = attend(ks_ref, vst_ref, sel_mask, 0)

    def win_mask(j, last):
        kpos = j * tk + krow
        mk = (kpos <= qpos) & (kpos > qpos - WINDOW)
        return mk, mk

    o_win = attend(kw_ref, vwt_ref, win_mask, jnp.maximum(i - WINDOW // tk, 0))

    gt = _sigmoid(gate_ref[...]).T
    for r in range(NSA_GROUP):
        def gate(branch):
            c = branch * 2 * NSA_GROUP + 2 * r
            return by_row_half(gt[c:c + 1, :], gt[c + 1:c + 2, :])
        o = gate(0) * occ_ref[r] + gate(1) * o_sel[r] + gate(2) * o_win[r]
        o_refs[r][...] = o.T.astype(o_refs[r].dtype)


def _nsa_attn(qn, qr, kc_bf, vct_bf, ks_bf, vst3, kw_bf, vwt3, gates, *, batch, seq):
    m = qn[0].shape[0]
    tq = KEY_BLOCK
    nq = seq // tq
    n_cmp = seq // CMP_BLOCK
    n_sel = seq // SEL_BLOCK
    qspec = pl.BlockSpec((tq, LANES), lambda b, p, i: (b * nq + i, p))
    kspec = pl.BlockSpec((seq, LANES), lambda b, p, i: (b, p))
    vspec = pl.BlockSpec((nq, LANES, tq), lambda b, p, i: (b, p, 0))
    return pl.pallas_call(
        _nsa_attn_kernel,
        grid=(batch, KV_WIDTH // LANES, nq),
        in_specs=[qspec] * 6 + [pl.BlockSpec((n_cmp, LANES), lambda b, p, i: (b, p)),
                                pl.BlockSpec((LANES, n_cmp), lambda b, p, i: (p, b)),
                                kspec, vspec, kspec, vspec, qspec],
        out_specs=[qspec] * 3,
        out_shape=[jax.ShapeDtypeStruct((m, KV_WIDTH), BF)] * 3,
        scratch_shapes=[pltpu.VMEM((tq // LANES, n_cmp, LANES), F32), pltpu.VMEM((n_sel, tq), F32),
                        pltpu.VMEM((2, n_sel, tq), F32),
                        pltpu.VMEM((NSA_GROUP, LANES, tq), F32), pltpu.VMEM((NSA_GROUP, LANES, tq), F32)],
        compiler_params=_cp("parallel", "parallel", "arbitrary"),
        name="nsa_attn",
    )(*qn, *qr, kc_bf, vct_bf, ks_bf, vst3, kw_bf, vwt3, gates)


def _tile_rows(x, reps):
    return jnp.concatenate([x] * reps, axis=0)


def _block_diag_q(q, n_heads):
    t = q.shape[0]
    assert n_heads * t <= LANES and q.shape[1] == n_heads * HEAD_DIM
    rows = _tile_rows(q, LANES // t)
    rh = jnp.right_shift(lax.broadcasted_iota(jnp.int32, rows.shape, 0), t.bit_length() - 1)
    lh = _head_of(lax.broadcasted_iota(jnp.int32, rows.shape, 1))
    return jnp.where(rh == lh, rows, 0.0).astype(BF)


def _gather_diag(o, n_heads, t):
    lh = _head_of(lax.broadcasted_iota(jnp.int32, (t, o.shape[1]), 1))
    out = jnp.zeros((t, o.shape[1]), F32)
    for h in range(n_heads):
        out = jnp.where(lh == h, o[h * t:(h + 1) * t, :], out)
    return out


def _flat_t(ref):
    x = ref[...]
    return x.reshape(x.shape[0] * x.shape[1], x.shape[2]).astype(BF)


def _pad_rows(pad_ref, x):
    pad_ref[...] = jnp.zeros_like(pad_ref)
    pad_ref[0:x.shape[0], :] = x
    return pad_ref[...].astype(BF)


def _row_query(shape, t):
    return jnp.bitwise_and(lax.broadcasted_iota(jnp.int32, shape, 0), t - 1)


def _softmax_lanes(s):
    e = jnp.exp2(s - jnp.max(s, axis=1, keepdims=True))
    return e / jnp.sum(e, axis=1, keepdims=True)


def _sb_dec_kernel(pps, n_steps, pt_ref, q_ref, kn_ref, vn_ref, *rest):
    kt_refs = rest[:pps]
    vt_refs = rest[pps:2 * pps]
    ut_ref, o_ref, qbd_ref, acc_ref, carry_ref, padk_ref, padv_ref = rest[2 * pps:]
    del pt_ref
    s = pl.program_id(1)
    t = q_ref.shape[0]
    ut = ut_ref[...]

    def blocks(y, new):
        lk = _sb_log_keep(y)
        if new:
            mask = lax.broadcasted_iota(jnp.int32, y.shape, 1) < _row_query(y.shape, t)
            lk = jnp.where(mask, lk, 0.0)
        c = carry_ref[...]
        ts = []
        for j in range(y.shape[1] // LANES):
            lkj = lk[:, j * LANES:(j + 1) * LANES]
            after = _dot(lkj.astype(BF), ut)
            ts.append((lkj - y[:, j * LANES:(j + 1) * LANES]) + after + c)
            c = c + jnp.broadcast_to(after[:, 0:1] + lkj[:, 0:1], c.shape)
        carry_ref[...] = c
        a = jnp.exp2(jnp.concatenate(ts, axis=1))
        if new:
            a = jnp.where(mask, a, 0.0)
        return a.astype(BF)

    @pl.when(s == 0)
    def _():
        qbd_ref[...] = _block_diag_q(q_ref[...], N_SELF_HEADS)
        carry_ref[...] = jnp.zeros_like(carry_ref)
        kn = _pad_rows(padk_ref, kn_ref[...])
        acc_ref[...] = _dot(blocks(_dot_nt(qbd_ref[...], kn), True), _pad_rows(padv_ref, vn_ref[...]))

    kt = jnp.concatenate([_flat_t(r) for r in kt_refs], axis=1)
    vt = jnp.concatenate([_flat_t(r) for r in vt_refs], axis=1)
    acc_ref[...] += _dot_nt(blocks(_dot(qbd_ref[...], kt), False), vt)

    @pl.when(s == n_steps - 1)
    def _():
        o_ref[...] = _gather_diag(acc_ref[...], N_SELF_HEADS, t)


def _sb_decode(pt, q, k_new, v_new, pool_kt, pool_vt, ut_bf, *, layer, n_pages, pps):
    m = q.shape[0]
    t = m // (pt.shape[0] // n_pages)
    db = m // t
    n_steps = n_pages // pps
    row = pl.BlockSpec((t, SELF_WIDTH), lambda b, s, pt: (b, 0))

    def page_spec(j):
        return pl.BlockSpec((None, None, N_SELF_HEADS, HEAD_DIM, PAGE_SIZE),
                            lambda b, s, pt: (layer, pt[b * n_pages + n_pages - 1 - (s * pps + j)], 0, 0, 0))

    pages = [page_spec(j) for j in range(pps)]
    return pl.pallas_call(
        functools.partial(_sb_dec_kernel, pps, n_steps),
        grid_spec=pltpu.PrefetchScalarGridSpec(
            num_scalar_prefetch=1,
            grid=(db, n_steps),
            in_specs=[row, row, row] + pages + pages + [pl.BlockSpec((LANES, LANES), lambda b, s, pt: (0, 0))],
            out_specs=row,
            scratch_shapes=[pltpu.VMEM((LANES, SELF_WIDTH), BF), pltpu.VMEM((LANES, SELF_WIDTH), F32),
                            pltpu.VMEM((LANES, LANES), F32), pltpu.VMEM((LANES, SELF_WIDTH), F32),
                            pltpu.VMEM((LANES, SELF_WIDTH), F32)]),
        out_shape=jax.ShapeDtypeStruct((m, SELF_WIDTH), F32),
        compiler_params=_cp("parallel", "arbitrary"),
        name="sb_decode",
    )(pt, q, k_new, v_new, *([pool_kt] * pps), *([pool_vt] * pps), ut_bf)


def _mem_dec_kernel(q_ref, kt_ref, vt_ref, o_ref):
    t = q_ref.shape[0]
    qbd = _block_diag_q(q_ref[...], N_MEM_HEADS)
    p = _softmax_lanes(_dot(qbd, _flat_t(kt_ref)))
    o_ref[...] = _gather_diag(_dot_nt(p.astype(BF), _flat_t(vt_ref)), N_MEM_HEADS, t)


def _mem_decode(qm, cache_kt, cache_vt, *, layer, t):
    m = qm.shape[0]
    row = pl.BlockSpec((t, MEM_WIDTH), lambda b: (b, 0))
    cache = pl.BlockSpec((None, None, N_MEM_HEADS, HEAD_DIM, N_MEM), lambda b: (layer, b, 0, 0, 0))
    return pl.pallas_call(
        _mem_dec_kernel,
        grid=(m // t,),
        in_specs=[row, cache, cache],
        out_specs=row,
        out_shape=jax.ShapeDtypeStruct((m, MEM_WIDTH), F32),
        compiler_params=_cp("parallel"),
        name="mem_decode",
    )(qm, cache_kt, cache_vt)


def _cmp_dec_kernel(n_pages, pt_ref, *refs):
    kp = refs[:n_pages]
    vp = refs[n_pages:2 * n_pages]
    pek_ref, pev_ref, wk_ref, wv_ref, ok_ref, ov_ref, xa_ref, xb_ref = refs[2 * n_pages:]
    del pt_ref
    rows = kp[0].shape[2]
    for pages, pe_ref, w_ref, o_ref in ((kp, pek_ref, wk_ref, ok_ref), (vp, pev_ref, wv_ref, ov_ref)):
        for p in range(n_pages):
            x = pages[p][...]
            xt = x.reshape(KV_WIDTH, rows).T
            xa_ref[p * rows:(p + 1) * rows, :] = xt[:, :LANES]
            xb_ref[p * rows:(p + 1) * rows, :] = xt[:, LANES:]
        o_ref[...] = _compress_rows(xa_ref, xb_ref, pe_ref, w_ref, o_ref.shape[0]).astype(o_ref.dtype)


def _cmp_decode(pt, pool_kt, pool_vt, pek2, pev2, wk2, wv2, *, layer, n_pages, db):
    n_cmp = n_pages * PAGE_SIZE // CMP_BLOCK

    def page_spec(p):
        return pl.BlockSpec((None, None, NSA_KV_HEADS, HEAD_DIM, PAGE_SIZE),
                            lambda b, pt: (layer, pt[b * n_pages + p], 0, 0, 0))

    pages = [page_spec(p) for p in range(n_pages)]
    const2 = lambda b, pt: (0, 0)
    const3 = lambda b, pt: (0, 0, 0)
    out = pl.BlockSpec((None, n_cmp, KV_WIDTH), lambda b, pt: (b, 0, 0))
    return pl.pallas_call(
        functools.partial(_cmp_dec_kernel, n_pages),
        grid_spec=pltpu.PrefetchScalarGridSpec(
            num_scalar_prefetch=1,
            grid=(db,),
            in_specs=pages + pages + [pl.BlockSpec(pek2.shape, const2), pl.BlockSpec(pev2.shape, const2),
                                      pl.BlockSpec(wk2.shape, const3), pl.BlockSpec(wv2.shape, const3)],
            out_specs=[out, out],
            scratch_shapes=[pltpu.VMEM((n_pages * PAGE_SIZE, LANES), F32)] * 2),
        out_shape=[jax.ShapeDtypeStruct((db, n_cmp, KV_WIDTH), BF)] * 2,
        compiler_params=_cp("parallel"),
        name="cmp_decode",
    )(pt, *([pool_kt] * n_pages), *([pool_vt] * n_pages), pek2, pev2, wk2, wv2)


def _nsa_dec_kernel(n_pages, past, pt_ref, qn0, qn1, qn2, qr0, qr1, qr2, gate_ref, kc_ref, vc_ref,
                    ksn_ref, vsn_ref, kwn_ref, vwn_ref, wkt_ref, wvt_ref, pm_ref, *rest):
    kp = rest[:n_pages]
    vp = rest[n_pages:2 * n_pages]
    o0, o1, o2, score_ref, pada_ref, padb_ref = rest[2 * n_pages:]
    del pt_ref
    t = qn0.shape[0]
    t_bits = t.bit_length() - 1
    grp = NSA_KV_HEADS * t

    def build(q_refs):
        parts = []
        for r in range(NSA_GROUP):
            rows = _tile_rows(q_refs[r][...], NSA_KV_HEADS)
            rg = jnp.right_shift(lax.broadcasted_iota(jnp.int32, rows.shape, 0), t_bits)
            lg = _head_of(lax.broadcasted_iota(jnp.int32, rows.shape, 1))
            parts.append(jnp.where(rg == lg, rows, 0.0))
        parts.append(jnp.zeros((LANES - N_SELF_HEADS * t, KV_WIDTH), F32))
        return jnp.concatenate(parts, axis=0).astype(BF)

    qbn = build((qn0, qn1, qn2))
    qbr = build((qr0, qr1, qr2))

    def qpos_of(shape):
        return past + _row_query(shape, t)

    kc = kc_ref[...]
    n_cmp = kc.shape[0]
    shape_c = (LANES, n_cmp)
    c_mask = (lax.broadcasted_iota(jnp.int32, shape_c, 1) * CMP_BLOCK + (CMP_BLOCK - 1)) <= qpos_of(shape_c)
    p = _softmax_lanes(jnp.where(c_mask, _dot_nt(qbn, kc), NEG))
    live = lax.broadcasted_iota(jnp.int32, shape_c, 0) < N_SELF_HEADS * t
    p = jnp.where(c_mask & live, p, 0.0)
    o_cmp = _dot(p.astype(BF), vc_ref[...])

    imp = p[0:grp] + p[grp:2 * grp] + p[2 * grp:3 * grp]
    imp = _tile_rows(imp, LANES // grp)
    hi = imp.astype(BF)
    mid = (imp - hi.astype(F32)).astype(BF)
    lo = (imp - hi.astype(F32) - mid.astype(F32)).astype(BF)
    pm = pm_ref[...]
    blk_imp = _dot(hi, pm) + _dot(mid, pm) + _dot(lo, pm)
    blk = lax.broadcasted_iota(jnp.int32, (LANES, LANES), 1)
    cur_r = _head_of(qpos_of((LANES, LANES)))
    forced = (blk == 0) | (blk == cur_r) | (blk == cur_r - 1)
    score = jnp.where(forced, jnp.inf, jnp.where(blk <= cur_r, blk_imp, -jnp.inf))
    n_pad = score_ref.shape[0]
    score_ref[...] = score.T[0:n_pad, :]
    cur_l = _head_of(past + jnp.bitwise_and(lax.broadcasted_iota(jnp.int32, (1, LANES), 1), t - 1))
    sel_t = _topk_select(score_ref, -(-(past + t) // SEL_BLOCK), cur_l)
    sel = jnp.concatenate([sel_t, jnp.zeros((LANES - n_pad, LANES), F32)], axis=0).T

    kcol = lax.broadcasted_iota(jnp.int32, (LANES, LANES), 1)
    tq = _row_query((LANES, LANES), t)
    new_ok = (kcol <= tq) & (kcol < t)
    per_page = PAGE_SIZE // SEL_BLOCK

    n_old = n_pages * PAGE_SIZE
    new_blk = n_pages * per_page
    kt_all = jnp.concatenate([_flat_t(r) for r in kp], axis=1)
    chosen = jnp.concatenate([jnp.broadcast_to(sel[:, b:b + 1], (LANES, SEL_BLOCK)) for b in range(new_blk)],
                             axis=1) > 0.5
    s_old = jnp.where(chosen, _dot(qbr, kt_all), NEG)
    ok = (jnp.broadcast_to(sel[:, new_blk:new_blk + 1], (LANES, LANES)) > 0.5) & new_ok
    s_new = jnp.where(ok, _dot_nt(qbr, _pad_rows(pada_ref, ksn_ref[...])), NEG)
    ps = _softmax_lanes(jnp.concatenate([s_old, s_new], axis=1))
    vt_all = jnp.concatenate([_flat_t(r) for r in vp], axis=1)
    o_sel = (_dot_nt(ps[:, :n_old].astype(BF), vt_all)
             + _dot(ps[:, n_old:].astype(BF), _pad_rows(padb_ref, vsn_ref[...])))

    n_win = wkt_ref.shape[2]
    shape_w = (LANES, n_win)
    wpos = (past - n_win) + lax.broadcasted_iota(jnp.int32, shape_w, 1)
    qpos_w = qpos_of(shape_w)
    s_old = jnp.where((wpos <= qpos_w) & (wpos > qpos_w - WINDOW), _dot(qbr, _flat_t(wkt_ref)), NEG)
    s_new = jnp.where(new_ok, _dot_nt(qbr, _pad_rows(pada_ref, kwn_ref[...])), NEG)
    pw = _softmax_lanes(jnp.concatenate([s_old, s_new], axis=1))
    o_win = (_dot_nt(pw[:, :n_win].astype(BF), _flat_t(wvt_ref))
             + _dot(pw[:, n_win:].astype(BF), _pad_rows(padb_ref, vwn_ref[...])))

    gates = _tile_rows(_sigmoid(gate_ref[...]), LANES // t)
    row = lax.broadcasted_iota(jnp.int32, gates.shape, 0)
    col = lax.broadcasted_iota(jnp.int32, gates.shape, 1)
    r_of = jnp.right_shift(row, t_bits + 2)
    g_of = jnp.bitwise_and(jnp.right_shift(row, t_bits), NSA_KV_HEADS - 1)
    base = jnp.right_shift(g_of, 1) * LANES + r_of * 2 + jnp.bitwise_and(g_of, 1)

    def gate(branch):
        return jnp.sum(jnp.where(col == base + branch * 2 * NSA_GROUP, gates, 0.0), axis=1, keepdims=True)

    o = gate(0) * o_cmp + gate(1) * o_sel + gate(2) * o_win
    for r, o_ref in enumerate((o0, o1, o2)):
        o_ref[...] = _gather_diag(o[r * grp:(r + 1) * grp, :], NSA_KV_HEADS, t)


def _nsa_decode(pt, qn, qr, gates, kc, vc, new_rows, win_kt, win_vt, pool_kt, pool_vt, pair_mat, *, layer, n_pages, t):
    m = qn[0].shape[0]
    db = m // t
    past = n_pages * PAGE_SIZE
    n_cmp = kc.shape[1]
    n_win = win_kt.shape[4]
    n_sel_pad = -(-(-(-(past + t) // SEL_BLOCK)) // 8) * 8
    row = pl.BlockSpec((t, KV_WIDTH), lambda b, pt: (b, 0))
    cmp_spec = pl.BlockSpec((None, n_cmp, KV_WIDTH), lambda b, pt: (b, 0, 0))
    win_spec = pl.BlockSpec((None, None, NSA_KV_HEADS, HEAD_DIM, n_win), lambda b, pt: (layer, b, 0, 0, 0))

    def page_spec(p):
        return pl.BlockSpec((None, None, NSA_KV_HEADS, HEAD_DIM, PAGE_SIZE),
                            lambda b, pt: (layer, pt[b * n_pages + p], 0, 0, 0))

    pages = [page_spec(p) for p in range(n_pages)]
    return pl.pallas_call(
        functools.partial(_nsa_dec_kernel, n_pages, past),
        grid_spec=pltpu.PrefetchScalarGridSpec(
            num_scalar_prefetch=1,
            grid=(db,),
            in_specs=[row] * 7 + [cmp_spec, cmp_spec] + [row] * 4 + [win_spec, win_spec]
            + [pl.BlockSpec(pair_mat.shape, lambda b, pt: (0, 0))] + pages + pages,
            out_specs=[row] * 3,
            scratch_shapes=[pltpu.VMEM((n_sel_pad, LANES), F32),
                            pltpu.VMEM((LANES, KV_WIDTH), F32), pltpu.VMEM((LANES, KV_WIDTH), F32)]),
        out_shape=[jax.ShapeDtypeStruct((m, KV_WIDTH), F32)] * 3,
        compiler_params=_cp("parallel"),
        name="nsa_decode",
    )(pt, *qn, *qr, gates, kc, vc, *new_rows, win_kt, win_vt, pair_mat,
      *([pool_kt] * n_pages), *([pool_vt] * n_pages))


NSA_IN = SELF_WIDTH + 6 * KV_WIDTH + 3 * N_SELF_HEADS + MEM_WIDTH
ROW_TILE = 512
SB_PAGES_PER_STEP = 8


def _nsa_head_order():
    return [NSA_GROUP * g + r for r in range(NSA_GROUP) for g in range(NSA_KV_HEADS)]


def _nsa_in_columns():
    q = [h * HEAD_DIM + d for h in _nsa_head_order() for d in range(HEAD_DIM)]
    kv = list(range(SELF_WIDTH, SELF_WIDTH + 6 * KV_WIDTH))
    g0 = SELF_WIDTH + 6 * KV_WIDTH
    gate = []
    for gp in range(KV_WIDTH // LANES):
        for c in range(LANES):
            if c < 3 * 2 * NSA_GROUP:
                branch, r, hf = c // (2 * NSA_GROUP), (c % (2 * NSA_GROUP)) // 2, c % 2
                gate.append(g0 + (NSA_GROUP * (2 * gp + hf) + r) * 3 + branch)
            else:
                gate.append(NSA_IN)
    qm = list(range(g0 + 3 * N_SELF_HEADS, NSA_IN))
    cols = np.asarray(q + kv + gate + qm, np.int32)
    assert cols.shape[0] == _NSA_COLS
    return cols


def _nsa_out_rows():
    o = [h * HEAD_DIM + d for h in _nsa_head_order() for d in range(HEAD_DIM)]
    return np.asarray(o + list(range(SELF_WIDTH, SELF_WIDTH + MEM_WIDTH)), np.int32)


def _block_diag2(w):
    eye = jnp.eye(LANES // HEAD_DIM, dtype=w.dtype)
    return jnp.einsum("gh,lde->lgdhe", eye, w).reshape(w.shape[0], LANES, LANES).astype(BF)


def kernel(x_prompt, x_sample, mem_prompt, cache_sb_k, cache_sb_v, cache_nsa_cmp_k, cache_nsa_cmp_v, cache_nsa_sel_k, cache_nsa_sel_v, cache_nsa_win_k, cache_nsa_win_v, cache_mem_k, cache_mem_v, page_table, ln_mix_pre, ln_mix_post, ln_ffn_pre, ln_ffn_post, ln_mem, w_in_a, w_in_b, w_cmp_k, w_cmp_v, pe_cmp_k, pe_cmp_v, w_out, w_mem_kv, w_gate_up, w_down):
    bsz, seq, d = x_prompt.shape
    db, ds, _ = x_sample.shape
    n_pages = page_table.shape[1]
    past = n_pages * PAGE_SIZE
    depth = w_out.shape[0]
    n_win = cache_nsa_win_k.shape[2]
    tm_p = min(ROW_TILE, bsz * seq)
    tm_s = min(ROW_TILE, db * ds)
    assert d == D_MODEL and seq % KEY_BLOCK == 0 and seq % tm_p == 0 and (db * ds) % tm_s == 0 and tm_s % ds == 0
    assert ds & (ds - 1) == 0 and N_SELF_HEADS * ds <= LANES and ds < CMP_BLOCK
    assert past % SEL_BLOCK == 0 and past + ds <= past + SEL_BLOCK and n_win % LANES == 0 and n_win <= past
    assert n_pages % SB_PAGES_PER_STEP == 0 and seq >= WINDOW

    h_p = x_prompt.reshape(bsz * seq, d)
    h_s = x_sample.reshape(db * ds, d)
    mem = mem_prompt.reshape(bsz * N_MEM, d)
    pt = page_table.reshape(-1).astype(jnp.int32)
    u256 = jnp.triu(jnp.ones((KEY_BLOCK, KEY_BLOCK), BF), 1)
    ut128 = jnp.tril(jnp.ones((LANES, LANES), BF), -1)
    pair_np = np.zeros((past // CMP_BLOCK, LANES), np.float32)
    pair_np[np.arange(past // CMP_BLOCK), np.arange(past // CMP_BLOCK) // (SEL_BLOCK // CMP_BLOCK)] = 1.0
    pair_mat = jnp.asarray(pair_np, BF)
    tmin = lambda c: jnp.transpose(c, (0, 1, 3, 4, 2))
    sbk_t, sbv_t = tmin(cache_sb_k), tmin(cache_sb_v)
    cmpk_t, cmpv_t = tmin(cache_nsa_cmp_k), tmin(cache_nsa_cmp_v)
    selk_t, selv_t = tmin(cache_nsa_sel_k), tmin(cache_nsa_sel_v)
    wink_t, winv_t = tmin(cache_nsa_win_k), tmin(cache_nsa_win_v)
    memk_t, memv_t = tmin(cache_mem_k), tmin(cache_mem_v)
    rope_p = _rope_tables(jnp.arange(seq, dtype=jnp.int32))
    rope_s = _rope_tables(past + jnp.arange(tm_s, dtype=jnp.int32) % ds)

    nsa_cols = _nsa_in_columns()
    nsa_rows = _nsa_out_rows()

    sb_p, sb_s = [[], []], [[], []]
    nsa_p, nsa_s = [[] for _ in range(6)], [[] for _ in range(6)]
    memk_p, memv_p = [], []
    for i in range(depth):
        j = i // 2
        g_pre = ln_mix_pre[i][None]
        tail_w = (ln_mix_post[i][None], ln_ffn_pre[i][None], ln_ffn_post[i][None],
                  w_gate_up[i].astype(BF), w_down[i].astype(BF))
        wkv = w_mem_kv[i]
        mk, mv, mk_bf, mvt_bf = _memkv(mem, ln_mem[i][None], wkv.astype(BF), wkv[:, MEM_WIDTH:].T.astype(BF))
        memk_p.append(mk.reshape(bsz, N_MEM, N_MEM_HEADS, HEAD_DIM))
        memv_p.append(mv.reshape(bsz, N_MEM, N_MEM_HEADS, HEAD_DIM))
        if i % 2 == 0:
            w = w_in_a[j]
            w_bf = w.astype(BF)
            wvt = w[:, 2 * SELF_WIDTH:3 * SELF_WIDTH].T.astype(BF)
            q, k, v, kb, qm, vt3 = _sb_inproj(h_p, g_pre, w_bf, wvt, tm=tm_p, with_vt=True, q_dtype=BF)
            o = _sb_attn(q, kb, vt3, u256, batch=bsz, seq=seq)
            om = _mem_attn(qm, mk_bf, mvt_bf, batch=bsz, seq=seq)
            qs, ks, vs, _, qms = _sb_inproj(h_s, g_pre, w_bf, wvt, tm=tm_s, with_vt=False, q_dtype=F32)
            o_s = _sb_decode(pt, qs, ks, vs, sbk_t, sbv_t, ut128, layer=j, n_pages=n_pages,
                             pps=SB_PAGES_PER_STEP)
            om_s = _mem_decode(qms, memk_t, memv_t, layer=i, t=ds)
            wo = w_out[i].astype(BF)
            mix_p, mix_s = [o, om], [o_s, om_s]
            for lst, a, shape in ((sb_p, (k, v), (bsz, seq)), (sb_s, (ks, vs), (db, ds))):
                for n in range(2):
                    lst[n].append(a[n].reshape(*shape, N_SELF_HEADS, HEAD_DIM))
        else:
            w = w_in_b[j]
            w_bf = jnp.concatenate([w, jnp.zeros((d, 1), w.dtype)], axis=1)[:, nsa_cols].astype(BF)
            vs0 = SELF_WIDTH + 3 * KV_WIDTH
            vw0 = SELF_WIDTH + 5 * KV_WIDTH
            wt = jnp.concatenate([w[:, vs0:vs0 + KV_WIDTH], w[:, vw0:vw0 + KV_WIDTH]], axis=1).T.astype(BF)
            pek2 = jnp.tile(pe_cmp_k[j], (1, LANES // HEAD_DIM))
            pev2 = jnp.tile(pe_cmp_v[j], (1, LANES // HEAD_DIM))
            wck2 = _block_diag2(w_cmp_k[j])
            wcv2 = _block_diag2(w_cmp_v[j])
            outs = _nsa_inproj(h_p, g_pre, w_bf, wt, *rope_p, tm=tm_p, with_vt=True, q_dtype=BF)
            qn, qr, rows_p = outs[0:3], outs[3:6], outs[6:12]
            ksb, kwb, gates, qm, vst3, vwt3 = outs[12:18]
            kc_bf, _ = _compress(rows_p[0], pek2, wck2, rows=seq)
            _, vct_bf = _compress(rows_p[1], pev2, wcv2, rows=seq)
            o3 = _nsa_attn(qn, qr, kc_bf, vct_bf, ksb, vst3, kwb, vwt3, gates, batch=bsz, seq=seq)
            om = _mem_attn(qm, mk_bf, mvt_bf, batch=bsz, seq=seq)
            outs = _nsa_inproj(h_s, g_pre, w_bf, wt, *rope_s, tm=tm_s, with_vt=False, q_dtype=F32)
            qn_s, qr_s, rows_s = outs[0:3], outs[3:6], outs[6:12]
            gates_s, qms = outs[14], outs[15]
            kc_s, vc_s = _cmp_decode(pt, cmpk_t, cmpv_t, pek2, pev2, wck2, wcv2, layer=j, n_pages=n_pages, db=db)
            o3_s = _nsa_decode(pt, qn_s, qr_s, gates_s, kc_s, vc_s, rows_s[2:6], wink_t, winv_t, selk_t, selv_t,
                               pair_mat, layer=j, n_pages=n_pages, t=ds)
            om_s = _mem_decode(qms, memk_t, memv_t, layer=i, t=ds)
            wo = w_out[i][nsa_rows].astype(BF)
            mix_p, mix_s = [*o3, om], [*o3_s, om_s]
            for n in range(6):
                rp = rows_p[n].reshape(bsz, seq, NSA_KV_HEADS, HEAD_DIM)
                nsa_p[n].append(rp[:, seq - min(WINDOW, seq):] if n >= 4 else rp)
                nsa_s[n].append(rows_s[n].reshape(db, ds, NSA_KV_HEADS, HEAD_DIM))
        h_p = _layer_tail(h_p, mix_p, wo, *tail_w, tm=tm_p)
        h_s = _layer_tail(h_s, mix_s, wo, *tail_w, tm=tm_s)

    st = lambda rows: jnp.stack(rows, axis=0)
    return (h_p.reshape(bsz, seq, d), h_s.reshape(db, ds, d),
            st(sb_p[0]), st(sb_p[1]),
            *[st(x) for x in nsa_p],
            st(memk_p), st(memv_p),
            st(sb_s[0]), st(sb_s[1]),
            *[st(x) for x in nsa_s])
```

```python
import functools

import jax
import jax.numpy as jnp
import numpy as np
from jax import lax
from jax.experimental import pallas as pl
from jax.experimental.pallas import tpu as pltpu

F32 = jnp.float32
BF = jnp.bfloat16

D_MODEL = 1024
HEAD_DIM = 64
N_SELF_HEADS = 12
N_MEM_HEADS = 4
SELF_WIDTH = N_SELF_HEADS * HEAD_DIM
MEM_WIDTH = N_MEM_HEADS * HEAD_DIM
N_MEM = 256
NSA_KV_HEADS = 4
NSA_GROUP = N_SELF_HEADS // NSA_KV_HEADS
KV_WIDTH = NSA_KV_HEADS * HEAD_DIM
CMP_BLOCK = 32
SEL_BLOCK = 64
SEL_TOPK = 16
WINDOW = 512
PAGE_SIZE = 128
ROPE_THETA = 500000.0
ROT_DIM = HEAD_DIM // 4
EPS = 1e-6
NEG = -1e30
SCALE = HEAD_DIM ** -0.5
Q_SCALE = SCALE * 1.4426950408889634

LANES = 128
KEY_BLOCK = 256
VMEM_LIMIT = 56 * 1024 * 1024


def _cp(*sem):
    return pltpu.CompilerParams(dimension_semantics=sem, vmem_limit_bytes=VMEM_LIMIT)


def _dot(a, b):
    return jnp.dot(a, b, preferred_element_type=F32)


def _dot_nt(a, b):
    return lax.dot_general(a, b, (((1,), (1,)), ((), ())), preferred_element_type=F32)


def _dot_tn(a, b):
    return lax.dot_general(a, b, (((0,), (0,)), ((), ())), preferred_element_type=F32)


def _rms(x, g):
    return x * lax.rsqrt(jnp.mean(x * x, axis=-1, keepdims=True) + EPS) * g


def _sigmoid(x):
    return 1.0 / (1.0 + jnp.exp(-x))


def _neg_abs(y):
    bits = lax.bitcast_convert_type(y, jnp.int32) | jnp.int32(-2 ** 31)
    return lax.bitcast_convert_type(bits, F32)


def _sb_log_keep(y):
    return jnp.minimum(y, 0.0) - jnp.log2(1.0 + jnp.exp2(_neg_abs(y)))


def _head_of(idx):
    return jnp.right_shift(idx, 6)


def _const_spec(shape):
    nd = len(shape)
    return pl.BlockSpec(shape, lambda *_: (0,) * nd, pipeline_mode=pl.Buffered(1))


def _rope128(x, c, s):
    lane = lax.broadcasted_iota(jnp.int32, x.shape, 1)
    up = pltpu.roll(x, LANES - ROT_DIM // 2, axis=1)
    dn = pltpu.roll(x, ROT_DIM // 2, axis=1)
    sw = jnp.where(jnp.bitwise_and(lane, HEAD_DIM - 1) < ROT_DIM // 2, up, dn)
    return x * c + sw * s


def _rope_tables(pos):
    half = ROT_DIM // 2
    inv = ROPE_THETA ** (-jnp.arange(half, dtype=F32) * 2.0 / ROT_DIM)
    ang = pos.astype(F32)[:, None] * inv[None, :]
    cos, sin = jnp.cos(ang), jnp.sin(ang)
    n = pos.shape[0]
    c64 = jnp.concatenate([cos, cos, jnp.ones((n, HEAD_DIM - ROT_DIM), F32)], axis=1)
    s64 = jnp.concatenate([-sin, sin, jnp.zeros((n, HEAD_DIM - ROT_DIM), F32)], axis=1)
    return jnp.tile(c64, (1, LANES // HEAD_DIM)), jnp.tile(s64, (1, LANES // HEAD_DIM))


def _store_vt3(vt3_ref, vt):
    for c in range(vt3_ref.shape[0]):
        vt3_ref[c] = vt[:, c * KEY_BLOCK:(c + 1) * KEY_BLOCK].astype(BF)


def _sb_inproj_kernel(prompt, x_ref, g_ref, w_ref, wt_ref, q_ref, qm_ref, *outs):
    xn = _rms(x_ref[...], g_ref[...]).astype(BF)
    q_ref[...] = (_dot(xn, w_ref[:, 0:SELF_WIDTH]) * -Q_SCALE).astype(q_ref.dtype)
    qm_ref[...] = (_dot(xn, w_ref[:, 3 * SELF_WIDTH:3 * SELF_WIDTH + MEM_WIDTH]) * Q_SCALE).astype(qm_ref.dtype)
    k = _dot(xn, w_ref[:, SELF_WIDTH:2 * SELF_WIDTH])
    if prompt:
        kb_ref, kt_ref, vt_ref, vt3_ref = outs
        kb_ref[...] = k.astype(BF)
        kt_ref[...] = _dot_nt(wt_ref[0:SELF_WIDTH, :], xn)
        vt = _dot_nt(wt_ref[SELF_WIDTH:2 * SELF_WIDTH, :], xn)
        vt_ref[...] = vt
        _store_vt3(vt3_ref, vt)
    else:
        k_ref, v_ref = outs
        k_ref[...] = k
        v_ref[...] = _dot(xn, w_ref[:, 2 * SELF_WIDTH:3 * SELF_WIDTH])


def _sb_inproj(x, g, w_bf, wt_bf, *, tm, seq):
    m = x.shape[0]
    prompt = seq is not None
    row = lambda i: (i, 0)
    qdt = BF if prompt else F32
    out_shape = [jax.ShapeDtypeStruct((m, SELF_WIDTH), qdt), jax.ShapeDtypeStruct((m, MEM_WIDTH), qdt)]
    out_specs = [pl.BlockSpec((tm, SELF_WIDTH), row), pl.BlockSpec((tm, MEM_WIDTH), row)]
    if prompt:
        per_seq = seq // tm
        tmaj = lambda i: (i // per_seq, 0, i % per_seq)
        out_shape += [jax.ShapeDtypeStruct((m, SELF_WIDTH), BF)]
        out_shape += [jax.ShapeDtypeStruct((m // seq, SELF_WIDTH, seq), F32)] * 2
        out_shape += [jax.ShapeDtypeStruct((m // KEY_BLOCK, SELF_WIDTH, KEY_BLOCK), BF)]
        out_specs += [pl.BlockSpec((tm, SELF_WIDTH), row)] + [pl.BlockSpec((None, SELF_WIDTH, tm), tmaj)] * 2
        out_specs += [pl.BlockSpec((tm // KEY_BLOCK, SELF_WIDTH, KEY_BLOCK), lambda i: (i, 0, 0))]
    else:
        out_shape += [jax.ShapeDtypeStruct((m, SELF_WIDTH), F32)] * 2
        out_specs += [pl.BlockSpec((tm, SELF_WIDTH), row)] * 2
    return pl.pallas_call(
        functools.partial(_sb_inproj_kernel, prompt),
        grid=(m // tm,),
        in_specs=[pl.BlockSpec((tm, D_MODEL), row), _const_spec((1, D_MODEL)),
                  _const_spec(w_bf.shape), _const_spec(wt_bf.shape)],
        out_specs=out_specs,
        out_shape=out_shape,
        compiler_params=_cp("parallel"),
        name="sb_inproj",
    )(x, g, w_bf, wt_bf)


_NSA_Q0 = 0
_NSA_KV0 = SELF_WIDTH
_NSA_G0 = _NSA_KV0 + 6 * KV_WIDTH
_NSA_QM0 = _NSA_G0 + 2 * LANES
_NSA_COLS = _NSA_QM0 + MEM_WIDTH


def _nsa_inproj_kernel(prompt, x_ref, g_ref, w_ref, wt_ref, c_ref, s_ref,
                       qn0, qn1, qn2, qr0, qr1, qr2, gate_ref, qm_ref, *outs):
    xn = _rms(x_ref[...], g_ref[...]).astype(BF)
    c = c_ref[...]
    s = s_ref[...]

    def rope256(y):
        return jnp.concatenate([_rope128(y[:, :LANES], c, s), _rope128(y[:, LANES:], c, s)], axis=1)

    for r, (qn_ref, qr_ref) in enumerate(((qn0, qr0), (qn1, qr1), (qn2, qr2))):
        q = _dot(xn, w_ref[:, r * KV_WIDTH:(r + 1) * KV_WIDTH]) * Q_SCALE
        qn_ref[...] = q.astype(qn_ref.dtype)
        qr_ref[...] = rope256(q).astype(qr_ref.dtype)
    gate_ref[...] = _dot(xn, w_ref[:, _NSA_G0:_NSA_G0 + 2 * LANES])
    qm_ref[...] = (_dot(xn, w_ref[:, _NSA_QM0:_NSA_QM0 + MEM_WIDTH]) * Q_SCALE).astype(qm_ref.dtype)

    def kv(j):
        return _dot(xn, w_ref[:, _NSA_KV0 + j * KV_WIDTH:_NSA_KV0 + (j + 1) * KV_WIDTH])

    def kvt(j):
        return _dot_nt(wt_ref[j * KV_WIDTH:(j + 1) * KV_WIDTH, :], xn)

    ks = rope256(kv(2))
    kw = rope256(kv(4))
    if prompt:
        kc_ref, vc_ref, ksb_ref, kwb_ref, kct_ref, vct_ref, kst_ref, vst_ref, kwt_ref, vwt_ref, vs3_ref, vw3_ref = outs
        kc_ref[...] = kv(0)
        vc_ref[...] = kv(1)
        ksb_ref[...] = ks.astype(BF)
        kwb_ref[...] = kw.astype(BF)
        kct_ref[...] = kvt(0)
        vct_ref[...] = kvt(1)
        kst_ref[...] = ks.T
        kwt_ref[...] = kw.T
        vst = kvt(2)
        vst_ref[...] = vst
        _store_vt3(vs3_ref, vst)
        vwt = kvt(3)
        vwt_ref[...] = vwt
        _store_vt3(vw3_ref, vwt)
    else:
        kc_ref, vc_ref, ks_ref, vs_ref, kw_ref, vw_ref = outs
        kc_ref[...] = kv(0)
        vc_ref[...] = kv(1)
        ks_ref[...] = ks
        vs_ref[...] = kv(3)
        kw_ref[...] = kw
        vw_ref[...] = kv(5)


def _nsa_inproj(x, g, w_bf, wt_bf, ctab, stab, *, tm, seq):
    m = x.shape[0]
    prompt = seq is not None
    row = lambda i: (i, 0)
    tbl_blocks = ctab.shape[0] // tm
    tbl = lambda i: (i % tbl_blocks, 0)
    qdt = BF if prompt else F32
    kvs = lambda dt: jax.ShapeDtypeStruct((m, KV_WIDTH), dt)
    rows = pl.BlockSpec((tm, KV_WIDTH), row)
    out_shape = [kvs(qdt)] * 6 + [kvs(F32), kvs(qdt)]
    out_specs = [rows] * 8
    if prompt:
        per_seq = seq // tm
        tmaj = lambda i: (i // per_seq, 0, i % per_seq)
        out_shape += [kvs(F32)] * 2 + [kvs(BF)] * 2 + [jax.ShapeDtypeStruct((m // seq, KV_WIDTH, seq), F32)] * 6
        out_shape += [jax.ShapeDtypeStruct((m // KEY_BLOCK, KV_WIDTH, KEY_BLOCK), BF)] * 2
        out_specs += [rows] * 4 + [pl.BlockSpec((None, KV_WIDTH, tm), tmaj)] * 6
        out_specs += [pl.BlockSpec((tm // KEY_BLOCK, KV_WIDTH, KEY_BLOCK), lambda i: (i, 0, 0))] * 2
    else:
        out_shape += [kvs(F32)] * 6
        out_specs += [rows] * 6
    return pl.pallas_call(
        functools.partial(_nsa_inproj_kernel, prompt),
        grid=(m // tm,),
        in_specs=[pl.BlockSpec((tm, D_MODEL), row), _const_spec((1, D_MODEL)),
                  _const_spec(w_bf.shape), _const_spec(wt_bf.shape),
                  pl.BlockSpec((tm, LANES), tbl), pl.BlockSpec((tm, LANES), tbl)],
        out_specs=out_specs,
        out_shape=out_shape,
        compiler_params=_cp("parallel"),
        name="nsa_inproj",
    )(x, g, w_bf, wt_bf, ctab, stab)


def _memkv_kernel(x_ref, g_ref, w_ref, wt_ref, kb_ref, kt_ref, vt_ref, vtb_ref):
    xn = _rms(x_ref[...], g_ref[...]).astype(BF)
    kb_ref[...] = _dot(xn, w_ref[:, :MEM_WIDTH]).astype(BF)
    kt_ref[...] = _dot_nt(wt_ref[0:MEM_WIDTH, :], xn)
    vt = _dot_nt(wt_ref[MEM_WIDTH:, :], xn)
    vt_ref[...] = vt
    vtb_ref[...] = vt.astype(BF)


def _memkv(x, g, w_bf, wt_bf):
    m = x.shape[0]
    row = lambda i: (i, 0)
    tmaj = pl.BlockSpec((None, MEM_WIDTH, N_MEM), lambda i: (i, 0, 0))
    return pl.pallas_call(
        _memkv_kernel,
        grid=(m // N_MEM,),
        in_specs=[pl.BlockSpec((N_MEM, D_MODEL), row), _const_spec((1, D_MODEL)),
                  _const_spec(w_bf.shape), _const_spec(wt_bf.shape)],
        out_specs=[pl.BlockSpec((N_MEM, MEM_WIDTH), row), tmaj, tmaj, pl.BlockSpec((MEM_WIDTH, N_MEM), lambda i: (0, i))],
        out_shape=[jax.ShapeDtypeStruct((m, MEM_WIDTH), BF)]
        + [jax.ShapeDtypeStruct((m // N_MEM, MEM_WIDTH, N_MEM), F32)] * 2 + [jax.ShapeDtypeStruct((MEM_WIDTH, m), BF)],
        compiler_params=_cp("parallel"),
        name="memkv",
    )(x, g, w_bf, wt_bf)


def _tail_kernel(n_mix, *refs):
    h_ref = refs[0]
    mix_refs = refs[1:1 + n_mix]
    wo_ref, gpost_ref, gpre_ref, gfpost_ref, wgu_ref, wd_ref, o_ref = refs[1 + n_mix:]
    y = None
    off = 0
    for mref in mix_refs:
        w = mref.shape[1]
        part = _dot(mref[...].astype(BF), wo_ref[off:off + w, :])
        y = part if y is None else y + part
        off += w
    h = h_ref[...] + _rms(y, gpost_ref[...])
    xn = _rms(h, gpre_ref[...]).astype(BF)
    d_ff = wd_ref.shape[0]
    acc = None
    for c0 in range(0, d_ff, KEY_BLOCK):
        gt = _dot(xn, wgu_ref[:, c0:c0 + KEY_BLOCK])
        up = _dot(xn, wgu_ref[:, d_ff + c0:d_ff + c0 + KEY_BLOCK])
        act = (gt * _sigmoid(gt) * up).astype(BF)
        part = _dot(act, wd_ref[c0:c0 + KEY_BLOCK, :])
        acc = part if acc is None else acc + part
    o_ref[...] = h + _rms(acc, gfpost_ref[...])


def _layer_tail(h, mixes, wo_bf, g_post, g_pre, g_fpost, wgu_bf, wd_bf, *, tm):
    m = h.shape[0]
    row = lambda i: (i, 0)
    const = lambda i: (0, 0)
    return pl.pallas_call(
        functools.partial(_tail_kernel, len(mixes)),
        grid=(m // tm,),
        in_specs=[pl.BlockSpec((tm, D_MODEL), row)] + [pl.BlockSpec((tm, x.shape[1]), row) for x in mixes]
        + [_const_spec(wo_bf.shape)] + [_const_spec((1, D_MODEL))] * 3
        + [_const_spec(wgu_bf.shape), _const_spec(wd_bf.shape)],
        out_specs=pl.BlockSpec((tm, D_MODEL), row),
        out_shape=jax.ShapeDtypeStruct((m, D_MODEL), F32),
        compiler_params=_cp("parallel"),
        name="layer_tail",
    )(h, *mixes, wo_bf, g_post, g_pre, g_fpost, wgu_bf, wd_bf)


def _half_masks(shape, axis):
    idx = lax.broadcasted_iota(jnp.int32, shape, axis)
    return idx < HEAD_DIM, idx >= HEAD_DIM


def _sb_attn_kernel(q_ref, k_ref, vt_ref, u_ref, o_ref, acc_ref, carry_ref):
    tq = q_ref.shape[0]
    i = pl.program_id(2)
    qi = q_ref[...]
    lo_l, hi_l = _half_masks(qi.shape, 1)
    zero = jnp.zeros_like(qi)
    qh = (jnp.where(lo_l, qi, zero), jnp.where(hi_l, qi, zero))
    acc_ref[...] = jnp.zeros_like(acc_ref)
    carry_ref[...] = jnp.zeros_like(carry_ref)
    u = u_ref[...]

    def run(blocks):
        carry = [carry_ref[h:h + 1, :] for h in range(2)]
        staged = []
        for j, diag in blocks:
            kb = k_ref[pl.ds(pl.multiple_of(j * tq, tq), tq), :]
            vt = vt_ref[j]
            lo_r, hi_r = _half_masks(vt.shape, 0)
            zv = jnp.zeros_like(vt)
            vth = (jnp.where(lo_r, vt, zv), jnp.where(hi_r, vt, zv))
            mask = None
            if diag:
                kr = lax.broadcasted_iota(jnp.int32, (tq, tq), 0)
                qc = lax.broadcasted_iota(jnp.int32, (tq, tq), 1)
                mask = kr < qc
            for h in range(2):
                y = _dot_nt(kb, qh[h])
                lk = _sb_log_keep(y)
                if diag:
                    lk = jnp.where(mask, lk, 0.0)
                after = _dot(u, lk.astype(BF))
                staged.append((h, y, lk, after, vth[h], mask))
        contrib = None
        for h, y, lk, after, vth_h, mask in staged:
            a = jnp.exp2((lk - y) + after + carry[h])
            if mask is not None:
                a = jnp.where(mask, a, 0.0)
            carry[h] = carry[h] + after[0:1, :] + lk[0:1, :]
            part = _dot(vth_h, a.astype(BF))
            contrib = part if contrib is None else contrib + part
        acc_ref[...] += contrib
        for h in range(2):
            carry_ref[h:h + 1, :] = carry[h]

    run([(i, True)])
    one = jnp.bitwise_and(i, 1)
    two = jnp.bitwise_and(i, 2)

    @pl.when(one == 1)
    def _():
        run([(i - 1, False)])

    @pl.when(two == 2)
    def _():
        run([(i - 1 - one, False), (i - 2 - one, False)])

    def body(jj, c):
        j = i - 1 - one - two - 4 * jj
        run([(j - u, False) for u in range(4)])
        return c

    lax.fori_loop(0, jnp.right_shift(i, 2), body, 0)
    o_ref[...] = acc_ref[...].T.astype(o_ref.dtype)


def _sb_attn(q_bf, k_bf, vt3, u_bf, *, batch, seq):
    m = q_bf.shape[0]
    tq = KEY_BLOCK
    nq = seq // tq
    return pl.pallas_call(
        _sb_attn_kernel,
        grid=(batch, SELF_WIDTH // LANES, nq),
        in_specs=[pl.BlockSpec((tq, LANES), lambda b, p, i: (b * nq + i, p)),
                  pl.BlockSpec((seq, LANES), lambda b, p, i: (b, p)),
                  pl.BlockSpec((nq, LANES, tq), lambda b, p, i: (b, p, 0)),
                  _const_spec((tq, tq))],
        out_specs=pl.BlockSpec((tq, LANES), lambda b, p, i: (b * nq + i, p)),
        out_shape=jax.ShapeDtypeStruct((m, SELF_WIDTH), BF),
        scratch_shapes=[pltpu.VMEM((LANES, tq), F32), pltpu.VMEM((8, tq), F32)],
        compiler_params=_cp("parallel", "parallel", "arbitrary"),
        name="sb_attn",
    )(q_bf, k_bf, vt3, u_bf)


def _mem_attn_kernel(q_ref, k_ref, vt_ref, o_ref):
    qi = q_ref[...]
    k = k_ref[...]
    vt = vt_ref[...]
    lane_h = _head_of(lax.broadcasted_iota(jnp.int32, qi.shape, 1))
    row_h = _head_of(lax.broadcasted_iota(jnp.int32, vt.shape, 0))
    acc = None
    for h in range(N_MEM_HEADS):
        s = _dot_nt(k, jnp.where(lane_h == h, qi, jnp.zeros_like(qi)))
        e = jnp.exp2(s - jnp.max(s, axis=0, keepdims=True))
        p = e / jnp.sum(e, axis=0, keepdims=True)
        part = _dot(jnp.where(row_h == h, vt, jnp.zeros_like(vt)), p.astype(BF))
        acc = part if acc is None else acc + part
    o_ref[...] = acc.T.astype(o_ref.dtype)


def _mem_attn(qm_bf, mk_bf, mvt_bf, *, batch, seq):
    m = qm_bf.shape[0]
    tq = KEY_BLOCK
    nq = seq // tq
    return pl.pallas_call(
        _mem_attn_kernel,
        grid=(batch, nq),
        in_specs=[pl.BlockSpec((tq, MEM_WIDTH), lambda b, i: (b * nq + i, 0)),
                  pl.BlockSpec((N_MEM, MEM_WIDTH), lambda b, i: (b, 0)),
                  pl.BlockSpec((MEM_WIDTH, N_MEM), lambda b, i: (0, b))],
        out_specs=pl.BlockSpec((tq, MEM_WIDTH), lambda b, i: (b * nq + i, 0)),
        out_shape=jax.ShapeDtypeStruct((m, MEM_WIDTH), BF),
        compiler_params=_cp("parallel", "parallel"),
        name="mem_attn",
    )(qm_bf, mk_bf, mvt_bf)


def _compress_rows(xa_ref, xb_ref, pe_ref, w_ref, nblk):
    acc = None
    for l in range(CMP_BLOCK):
        xs = [ref[pl.ds(l, nblk, stride=CMP_BLOCK), :] for ref in (xa_ref, xb_ref)]
        xl = jnp.concatenate(xs, axis=0) + pe_ref[l:l + 1, :]
        part = _dot(xl.astype(BF), w_ref[l])
        acc = part if acc is None else acc + part
    return jnp.concatenate([acc[:nblk], acc[nblk:]], axis=1)


def _compress_kernel(xa_ref, xb_ref, pe_ref, w_ref, o_ref, ot_ref):
    out = _compress_rows(xa_ref, xb_ref, pe_ref, w_ref, o_ref.shape[0])
    o_ref[...] = out.astype(o_ref.dtype)
    ot_ref[...] = out.T.astype(ot_ref.dtype)


def _compress(x, pe2, wbd2, *, rows):
    m = x.shape[0]
    nblk = rows // CMP_BLOCK
    return pl.pallas_call(
        _compress_kernel,
        grid=(m // rows,),
        in_specs=[pl.BlockSpec((rows, LANES), lambda i: (i, 0)), pl.BlockSpec((rows, LANES), lambda i: (i, 1)),
                  _const_spec(pe2.shape), _const_spec(wbd2.shape)],
        out_specs=[pl.BlockSpec((nblk, KV_WIDTH), lambda i: (i, 0)), pl.BlockSpec((KV_WIDTH, nblk), lambda i: (0, i))],
        out_shape=[jax.ShapeDtypeStruct((m // CMP_BLOCK, KV_WIDTH), BF),
                   jax.ShapeDtypeStruct((KV_WIDTH, m // CMP_BLOCK), BF)],
        compiler_params=_cp("parallel"),
        name="compress",
    )(x, x, pe2, wbd2)


def _topk_select(score_ref, n_blocks, cur):
    score = score_ref[...]
    blk = lax.broadcasted_iota(jnp.int32, score.shape, 0)

    def body(i, cnt):
        si = score_ref[pl.ds(i, 1), :]
        tie = jnp.where(blk > i, 1.0, 0.0)
        return cnt + jnp.where(si > score, 1.0, jnp.where(si == score, tie, 0.0))

    rank = lax.fori_loop(0, n_blocks, body, jnp.zeros(score.shape, F32))
    return jnp.where((rank < SEL_TOPK) & (blk <= cur), 1.0, 0.0).astype(F32)


def _nsa_attn_kernel(qn0, qn1, qn2, qr0, qr1, qr2, kc_ref, vct_ref, ks_ref, vst_ref, kw_ref, vwt_ref, gate_ref,
                     o0, o1, o2, imp_ref, score_ref, sel_ref, acc_ref, occ_ref):
    tq = qn0.shape[0]
    tk = tq
    i = pl.program_id(2)
    qn_refs = (qn0, qn1, qn2)
    qr_refs = (qr0, qr1, qr2)
    o_refs = (o0, o1, o2)
    qpos = i * tq + lax.broadcasted_iota(jnp.int32, (1, tq), 1)

    def halves(x):
        lo, hi = _half_masks(x.shape, 1)
        z = jnp.zeros_like(x)
        return jnp.where(lo, x, z), jnp.where(hi, x, z)

    def by_row_half(a, b):
        lo, _ = _half_masks((LANES, tq), 0)
        return jnp.where(lo, a, b)

    def vhalves(vt):
        lo, hi = _half_masks(vt.shape, 0)
        z = jnp.zeros_like(vt)
        return jnp.where(lo, vt, z), jnp.where(hi, vt, z)

    n_cmp = kc_ref.shape[0]
    kc = kc_ref[...]
    vct = vhalves(vct_ref[...])
    nrow = lax.broadcasted_iota(jnp.int32, (n_cmp, tq), 0)
    c_mask = (nrow * CMP_BLOCK + (CMP_BLOCK - 1)) <= qpos
    imp = [None, None]
    for r in range(NSA_GROUP):
        qh = halves(qn_refs[r][...])
        occ = None
        for hf in range(2):
            s = jnp.where(c_mask, _dot_nt(kc, qh[hf]), NEG)
            e = jnp.exp2(s - jnp.max(s, axis=0, keepdims=True))
            p = jnp.where(c_mask, e / jnp.sum(e, axis=0, keepdims=True), 0.0)
            imp[hf] = p if imp[hf] is None else imp[hf] + p
            part = _dot(vct[hf], p.astype(BF))
            occ = part if occ is None else occ + part
        occ_ref[r] = occ

    n_sel = sel_ref.shape[1]
    ratio = SEL_BLOCK // CMP_BLOCK
    blk = lax.broadcasted_iota(jnp.int32, (n_sel, tq), 0)
    cur = _head_of(qpos)
    forced = (blk == 0) | (blk == cur) | (blk == cur - 1)
    allowed = blk <= cur
    scores = []
    for hf in range(2):
        parts = []
        for c in range(tq // LANES):
            imp_ref[c] = imp[hf][:, c * LANES:(c + 1) * LANES]
            parts.append(imp_ref[c, pl.ds(0, n_sel, stride=ratio), :] + imp_ref[c, pl.ds(1, n_sel, stride=ratio), :])
        blk_imp = jnp.concatenate(parts, axis=1)
        scores.append(jnp.where(forced, jnp.inf, jnp.where(allowed, blk_imp, -jnp.inf)))
    for hf in range(2):
        score_ref[...] = scores[hf]
        sel_ref[hf] = _topk_select(score_ref, jnp.minimum(n_sel, (i + 1) * (tq // SEL_BLOCK)), cur)

    qr_h = [halves(qr_refs[r][...]) for r in range(NSA_GROUP)]
    krow = lax.broadcasted_iota(jnp.int32, (tk, tq), 0)
    n_sub = tk // SEL_BLOCK
    n_heads = 2 * NSA_GROUP

    def attend(k_ref, vt_ref, mask_of, j_lo):
        def scores(j, last):
            kb = k_ref[pl.ds(pl.multiple_of(j * tk, tk), tk), :]
            masks = mask_of(j, last)
            return [jnp.where(masks[h % 2], _dot_nt(kb, qr_h[h // 2][h % 2]), NEG) for h in range(n_heads)]

        def max_step(j, last, ms):
            s = scores(j, last)
            return tuple(jnp.maximum(ms[h], jnp.max(s[h], axis=0, keepdims=True)) for h in range(n_heads))

        ms = lax.fori_loop(j_lo, i, lambda j, c: max_step(j, False, c),
                           tuple(jnp.full((1, tq), NEG, F32) for _ in range(n_heads)))
        ms = max_step(i, True, ms)
        acc_ref[...] = jnp.zeros_like(acc_ref)

        def sum_step(j, last, ls):
            s = scores(j, last)
            vth = vhalves(vt_ref[j])
            ls = list(ls)
            for r in range(NSA_GROUP):
                contrib = None
                for hf in range(2):
                    h = 2 * r + hf
                    p = jnp.exp2(s[h] - ms[h])
                    ls[h] = ls[h] + jnp.sum(p, axis=0, keepdims=True)
                    part = _dot(vth[hf], p.astype(BF))
                    contrib = part if contrib is None else contrib + part
                acc_ref[r] += contrib
            return tuple(ls)

        ls = lax.fori_loop(j_lo, i, lambda j, c: sum_step(j, False, c),
                           tuple(jnp.zeros((1, tq), F32) for _ in range(n_heads)))
        ls = sum_step(i, True, ls)
        return [acc_ref[r] * by_row_half(1.0 / ls[2 * r], 1.0 / ls[2 * r + 1]) for r in range(NSA_GROUP)]

    def sel_mask(j, last):
        masks = []
        for hf in range(2):
            selv = jnp.concatenate(
                [jnp.broadcast_to(sel_ref[hf, pl.ds(j * n_sub + u, 1), :], (SEL_BLOCK, tq)) for u in range(n_sub)],
                axis=0)
            mk = selv > 0.5
            if last:
                mk = mk & ((j * tk + krow) <= qpos)
            masks.append(mk)
        return masks

    o_sel = attend(ks_ref, vst_ref, sel_mask, 0)

    def win_mask(j, last):
        kpos = j * tk + krow
        mk = (kpos <= qpos) & (kpos > qpos - WINDOW)
        return mk, mk

    o_win = attend(kw_ref, vwt_ref, win_mask, jnp.maximum(i - WINDOW // tk, 0))

    gt = _sigmoid(gate_ref[...]).T
    for r in range(NSA_GROUP):
        def gate(branch):
            c = branch * 2 * NSA_GROUP + 2 * r
            return by_row_half(gt[c:c + 1, :], gt[c + 1:c + 2, :])
        o = gate(0) * occ_ref[r] + gate(1) * o_sel[r] + gate(2) * o_win[r]
        o_refs[r][...] = o.T.astype(o_refs[r].dtype)


def _nsa_attn(qn, qr, kc_bf, vct_bf, ks_bf, vst3, kw_bf, vwt3, gates, *, batch, seq):
    m = qn[0].shape[0]
    tq = KEY_BLOCK
    nq = seq // tq
    n_cmp = seq // CMP_BLOCK
    n_sel = seq // SEL_BLOCK
    qspec = pl.BlockSpec((tq, LANES), lambda b, p, i: (b * nq + i, p))
    kspec = pl.BlockSpec((seq, LANES), lambda b, p, i: (b, p))
    vspec = pl.BlockSpec((nq, LANES, tq), lambda b, p, i: (b, p, 0))
    return pl.pallas_call(
        _nsa_attn_kernel,
        grid=(batch, KV_WIDTH // LANES, nq),
        in_specs=[qspec] * 6 + [pl.BlockSpec((n_cmp, LANES), lambda b, p, i: (b, p)),
                                pl.BlockSpec((LANES, n_cmp), lambda b, p, i: (p, b)),
                                kspec, vspec, kspec, vspec, qspec],
        out_specs=[qspec] * 3,
        out_shape=[jax.ShapeDtypeStruct((m, KV_WIDTH), BF)] * 3,
        scratch_shapes=[pltpu.VMEM((tq // LANES, n_cmp, LANES), F32), pltpu.VMEM((n_sel, tq), F32),
                        pltpu.VMEM((2, n_sel, tq), F32),
                        pltpu.VMEM((NSA_GROUP, LANES, tq), F32), pltpu.VMEM((NSA_GROUP, LANES, tq), F32)],
        compiler_params=_cp("parallel", "parallel", "arbitrary"),
        name="nsa_attn",
    )(*qn, *qr, kc_bf, vct_bf, ks_bf, vst3, kw_bf, vwt3, gates)


def _tile_rows(x, reps):
    return jnp.concatenate([x] * reps, axis=0)


def _block_diag_q(q, n_heads):
    t = q.shape[0]
    assert n_heads * t <= LANES and q.shape[1] == n_heads * HEAD_DIM
    rows = _tile_rows(q, LANES // t)
    rh = jnp.right_shift(lax.broadcasted_iota(jnp.int32, rows.shape, 0), t.bit_length() - 1)
    lh = _head_of(lax.broadcasted_iota(jnp.int32, rows.shape, 1))
    return jnp.where(rh == lh, rows, 0.0).astype(BF)


def _gather_diag(o, n_heads, t):
    lh = _head_of(lax.broadcasted_iota(jnp.int32, (t, o.shape[1]), 1))
    out = jnp.zeros((t, o.shape[1]), F32)
    for h in range(n_heads):
        out = jnp.where(lh == h, o[h * t:(h + 1) * t, :], out)
    return out


def _flat_t(ref):
    x = ref[...]
    return x.reshape(x.shape[0] * x.shape[1], x.shape[2]).astype(BF)


def _pad_rows(pad_ref, x):
    pad_ref[...] = jnp.zeros_like(pad_ref)
    pad_ref[0:x.shape[0], :] = x
    return pad_ref[...].astype(BF)


def _row_query(shape, t):
    return jnp.bitwise_and(lax.broadcasted_iota(jnp.int32, shape, 0), t - 1)


def _softmax_lanes(s):
    e = jnp.exp2(s - jnp.max(s, axis=1, keepdims=True))
    return e / jnp.sum(e, axis=1, keepdims=True)


def _sb_dec_kernel(pps, n_steps, pt_ref, q_ref, kn_ref, vn_ref, *rest):
    kt_refs = rest[:pps]
    vt_refs = rest[pps:2 * pps]
    ut_ref, o_ref, qbd_ref, acc_ref, carry_ref, padk_ref, padv_ref = rest[2 * pps:]
    del pt_ref
    s = pl.program_id(1)
    t = q_ref.shape[0]
    ut = ut_ref[...]

    def blocks(y, new):
        lk = _sb_log_keep(y)
        if new:
            mask = lax.broadcasted_iota(jnp.int32, y.shape, 1) < _row_query(y.shape, t)
            lk = jnp.where(mask, lk, 0.0)
        c = carry_ref[...]
        ts = []
        for j in range(y.shape[1] // LANES):
            lkj = lk[:, j * LANES:(j + 1) * LANES]
            after = _dot(lkj.astype(BF), ut)
            ts.append((lkj - y[:, j * LANES:(j + 1) * LANES]) + after + c)
            c = c + jnp.broadcast_to(after[:, 0:1] + lkj[:, 0:1], c.shape)
        carry_ref[...] = c
        a = jnp.exp2(jnp.concatenate(ts, axis=1))
        if new:
            a = jnp.where(mask, a, 0.0)
        return a.astype(BF)

    @pl.when(s == 0)
    def _():
        qbd_ref[...] = _block_diag_q(q_ref[...], N_SELF_HEADS)
        carry_ref[...] = jnp.zeros_like(carry_ref)
        kn = _pad_rows(padk_ref, kn_ref[...])
        acc_ref[...] = _dot(blocks(_dot_nt(qbd_ref[...], kn), True), _pad_rows(padv_ref, vn_ref[...]))

    kt = jnp.concatenate([_flat_t(r) for r in kt_refs], axis=1)
    vt = jnp.concatenate([_flat_t(r) for r in vt_refs], axis=1)
    acc_ref[...] += _dot_nt(blocks(_dot(qbd_ref[...], kt), False), vt)

    @pl.when(s == n_steps - 1)
    def _():
        o_ref[...] = _gather_diag(acc_ref[...], N_SELF_HEADS, t)


def _sb_decode(pt, q, k_new, v_new, pool_kt, pool_vt, ut_bf, *, layer, n_pages, pps):
    m = q.shape[0]
    t = m // (pt.shape[0] // n_pages)
    db = m // t
    n_steps = n_pages // pps
    row = pl.BlockSpec((t, SELF_WIDTH), lambda b, s, pt: (b, 0))

    def page_spec(j):
        return pl.BlockSpec((None, None, N_SELF_HEADS, HEAD_DIM, PAGE_SIZE),
                            lambda b, s, pt: (layer, pt[b * n_pages + n_pages - 1 - (s * pps + j)], 0, 0, 0))

    pages = [page_spec(j) for j in range(pps)]
    return pl.pallas_call(
        functools.partial(_sb_dec_kernel, pps, n_steps),
        grid_spec=pltpu.PrefetchScalarGridSpec(
            num_scalar_prefetch=1,
            grid=(db, n_steps),
            in_specs=[row, row, row] + pages + pages + [pl.BlockSpec((LANES, LANES), lambda b, s, pt: (0, 0))],
            out_specs=row,
            scratch_shapes=[pltpu.VMEM((LANES, SELF_WIDTH), BF), pltpu.VMEM((LANES, SELF_WIDTH), F32),
                            pltpu.VMEM((LANES, LANES), F32), pltpu.VMEM((LANES, SELF_WIDTH), F32),
                            pltpu.VMEM((LANES, SELF_WIDTH), F32)]),
        out_shape=jax.ShapeDtypeStruct((m, SELF_WIDTH), F32),
        compiler_params=_cp("parallel", "arbitrary"),
        name="sb_decode",
    )(pt, q, k_new, v_new, *([pool_kt] * pps), *([pool_vt] * pps), ut_bf)


def _mem_dec_kernel(q_ref, kt_ref, vt_ref, o_ref):
    t = q_ref.shape[0]
    qbd = _block_diag_q(q_ref[...], N_MEM_HEADS)
    p = _softmax_lanes(_dot(qbd, _flat_t(kt_ref)))
    o_ref[...] = _gather_diag(_dot_nt(p.astype(BF), _flat_t(vt_ref)), N_MEM_HEADS, t)


def _mem_decode(qm, cache_kt, cache_vt, *, layer, t):
    m = qm.shape[0]
    row = pl.BlockSpec((t, MEM_WIDTH), lambda b: (b, 0))
    cache = pl.BlockSpec((None, None, N_MEM_HEADS, HEAD_DIM, N_MEM), lambda b: (layer, b, 0, 0, 0))
    return pl.pallas_call(
        _mem_dec_kernel,
        grid=(m // t,),
        in_specs=[row, cache, cache],
        out_specs=row,
        out_shape=jax.ShapeDtypeStruct((m, MEM_WIDTH), F32),
        compiler_params=_cp("parallel"),
        name="mem_decode",
    )(qm, cache_kt, cache_vt)


def _cmp_dec_kernel(n_pages, pt_ref, *refs):
    kp = refs[:n_pages]
    vp = refs[n_pages:2 * n_pages]
    pek_ref, pev_ref, wk_ref, wv_ref, ok_ref, ov_ref, xa_ref, xb_ref = refs[2 * n_pages:]
    del pt_ref
    rows = kp[0].shape[2]
    for pages, pe_ref, w_ref, o_ref in ((kp, pek_ref, wk_ref, ok_ref), (vp, pev_ref, wv_ref, ov_ref)):
        for p in range(n_pages):
            x = pages[p][...]
            xt = x.reshape(KV_WIDTH, rows).T
            xa_ref[p * rows:(p + 1) * rows, :] = xt[:, :LANES]
            xb_ref[p * rows:(p + 1) * rows, :] = xt[:, LANES:]
        o_ref[...] = _compress_rows(xa_ref, xb_ref, pe_ref, w_ref, o_ref.shape[0]).astype(o_ref.dtype)


def _cmp_decode(pt, pool_kt, pool_vt, pek2, pev2, wk2, wv2, *, layer, n_pages, db):
    n_cmp = n_pages * PAGE_SIZE // CMP_BLOCK

    def page_spec(p):
        return pl.BlockSpec((None, None, NSA_KV_HEADS, HEAD_DIM, PAGE_SIZE),
                            lambda b, pt: (layer, pt[b * n_pages + p], 0, 0, 0))

    pages = [page_spec(p) for p in range(n_pages)]
    const2 = lambda b, pt: (0, 0)
    const3 = lambda b, pt: (0, 0, 0)
    out = pl.BlockSpec((None, n_cmp, KV_WIDTH), lambda b, pt: (b, 0, 0))
    return pl.pallas_call(
        functools.partial(_cmp_dec_kernel, n_pages),
        grid_spec=pltpu.PrefetchScalarGridSpec(
            num_scalar_prefetch=1,
            grid=(db,),
            in_specs=pages + pages + [pl.BlockSpec(pek2.shape, const2), pl.BlockSpec(pev2.shape, const2),
                                      pl.BlockSpec(wk2.shape, const3), pl.BlockSpec(wv2.shape, const3)],
            out_specs=[out, out],
            scratch_shapes=[pltpu.VMEM((n_pages * PAGE_SIZE, LANES), F32)] * 2),
        out_shape=[jax.ShapeDtypeStruct((db, n_cmp, KV_WIDTH), BF)] * 2,
        compiler_params=_cp("parallel"),
        name="cmp_decode",
    )(pt, *([pool_kt] * n_pages), *([pool_vt] * n_pages), pek2, pev2, wk2, wv2)


def _nsa_dec_kernel(n_pages, past, pt_ref, qn0, qn1, qn2, qr0, qr1, qr2, gate_ref, kc_ref, vc_ref,
                    ksn_ref, vsn_ref, kwn_ref, vwn_ref, wkt_ref, wvt_ref, pm_ref, *rest):
    kp = rest[:n_pages]
    vp = rest[n_pages:2 * n_pages]
    o0, o1, o2, score_ref, pada_ref, padb_ref = rest[2 * n_pages:]
    del pt_ref
    t = qn0.shape[0]
    t_bits = t.bit_length() - 1
    grp = NSA_KV_HEADS * t

    def build(q_refs):
        parts = []
        for r in range(NSA_GROUP):
            rows = _tile_rows(q_refs[r][...], NSA_KV_HEADS)
            rg = jnp.right_shift(lax.broadcasted_iota(jnp.int32, rows.shape, 0), t_bits)
            lg = _head_of(lax.broadcasted_iota(jnp.int32, rows.shape, 1))
            parts.append(jnp.where(rg == lg, rows, 0.0))
        parts.append(jnp.zeros((LANES - N_SELF_HEADS * t, KV_WIDTH), F32))
        return jnp.concatenate(parts, axis=0).astype(BF)

    qbn = build((qn0, qn1, qn2))
    qbr = build((qr0, qr1, qr2))

    def qpos_of(shape):
        return past + _row_query(shape, t)

    kc = kc_ref[...]
    n_cmp = kc.shape[0]
    shape_c = (LANES, n_cmp)
    c_mask = (lax.broadcasted_iota(jnp.int32, shape_c, 1) * CMP_BLOCK + (CMP_BLOCK - 1)) <= qpos_of(shape_c)
    p = _softmax_lanes(jnp.where(c_mask, _dot_nt(qbn, kc), NEG))
    live = lax.broadcasted_iota(jnp.int32, shape_c, 0) < N_SELF_HEADS * t
    p = jnp.where(c_mask & live, p, 0.0)
    o_cmp = _dot(p.astype(BF), vc_ref[...])

    imp = p[0:grp] + p[grp:2 * grp] + p[2 * grp:3 * grp]
    imp = _tile_rows(imp, LANES // grp)
    hi = imp.astype(BF)
    mid = (imp - hi.astype(F32)).astype(BF)
    lo = (imp - hi.astype(F32) - mid.astype(F32)).astype(BF)
    pm = pm_ref[...]
    blk_imp = _dot(hi, pm) + _dot(mid, pm) + _dot(lo, pm)
    blk = lax.broadcasted_iota(jnp.int32, (LANES, LANES), 1)
    cur_r = _head_of(qpos_of((LANES, LANES)))
    forced = (blk == 0) | (blk == cur_r) | (blk == cur_r - 1)
    score = jnp.where(forced, jnp.inf, jnp.where(blk <= cur_r, blk_imp, -jnp.inf))
    n_pad = score_ref.shape[0]
    score_ref[...] = score.T[0:n_pad, :]
    cur_l = _head_of(past + jnp.bitwise_and(lax.broadcasted_iota(jnp.int32, (1, LANES), 1), t - 1))
    sel_t = _topk_select(score_ref, -(-(past + t) // SEL_BLOCK), cur_l)
    sel = jnp.concatenate([sel_t, jnp.zeros((LANES - n_pad, LANES), F32)], axis=0).T

    kcol = lax.broadcasted_iota(jnp.int32, (LANES, LANES), 1)
    tq = _row_query((LANES, LANES), t)
    new_ok = (kcol <= tq) & (kcol < t)
    per_page = PAGE_SIZE // SEL_BLOCK

    n_old = n_pages * PAGE_SIZE
    new_blk = n_pages * per_page
    kt_all = jnp.concatenate([_flat_t(r) for r in kp], axis=1)
    chosen = jnp.concatenate([jnp.broadcast_to(sel[:, b:b + 1], (LANES, SEL_BLOCK)) for b in range(new_blk)],
                             axis=1) > 0.5
    s_old = jnp.where(chosen, _dot(qbr, kt_all), NEG)
    ok = (jnp.broadcast_to(sel[:, new_blk:new_blk + 1], (LANES, LANES)) > 0.5) & new_ok
    s_new = jnp.where(ok, _dot_nt(qbr, _pad_rows(pada_ref, ksn_ref[...])), NEG)
    ps = _softmax_lanes(jnp.concatenate([s_old, s_new], axis=1))
    vt_all = jnp.concatenate([_flat_t(r) for r in vp], axis=1)
    o_sel = (_dot_nt(ps[:, :n_old].astype(BF), vt_all)
             + _dot(ps[:, n_old:].astype(BF), _pad_rows(padb_ref, vsn_ref[...])))

    n_win = wkt_ref.shape[2]
    shape_w = (LANES, n_win)
    wpos = (past - n_win) + lax.broadcasted_iota(jnp.int32, shape_w, 1)
    qpos_w = qpos_of(shape_w)
    s_old = jnp.where((wpos <= qpos_w) & (wpos > qpos_w - WINDOW), _dot(qbr, _flat_t(wkt_ref)), NEG)
    s_new = jnp.where(new_ok, _dot_nt(qbr, _pad_rows(pada_ref, kwn_ref[...])), NEG)
    pw = _softmax_lanes(jnp.concatenate([s_old, s_new], axis=1))
    o_win = (_dot_nt(pw[:, :n_win].astype(BF), _flat_t(wvt_ref))
             + _dot(pw[:, n_win:].astype(BF), _pad_rows(padb_ref, vwn_ref[...])))

    gates = _tile_rows(_sigmoid(gate_ref[...]), LANES // t)
    row = lax.broadcasted_iota(jnp.int32, gates.shape, 0)
    col = lax.broadcasted_iota(jnp.int32, gates.shape, 1)
    r_of = jnp.right_shift(row, t_bits + 2)
    g_of = jnp.bitwise_and(jnp.right_shift(row, t_bits), NSA_KV_HEADS - 1)
    base = jnp.right_shift(g_of, 1) * LANES + r_of * 2 + jnp.bitwise_and(g_of, 1)

    def gate(branch):
        return jnp.sum(jnp.where(col == base + branch * 2 * NSA_GROUP, gates, 0.0), axis=1, keepdims=True)

    o = gate(0) * o_cmp + gate(1) * o_sel + gate(2) * o_win
    for r, o_ref in enumerate((o0, o1, o2)):
        o_ref[...] = _gather_diag(o[r * grp:(r + 1) * grp, :], NSA_KV_HEADS, t)


def _nsa_decode(pt, qn, qr, gates, kc, vc, new_rows, win_kt, win_vt, pool_kt, pool_vt, pair_mat, *, layer, n_pages, t):
    m = qn[0].shape[0]
    db = m // t
    past = n_pages * PAGE_SIZE
    n_cmp = kc.shape[1]
    n_win = win_kt.shape[4]
    n_sel_pad = -(-(-(-(past + t) // SEL_BLOCK)) // 8) * 8
    row = pl.BlockSpec((t, KV_WIDTH), lambda b, pt: (b, 0))
    cmp_spec = pl.BlockSpec((None, n_cmp, KV_WIDTH), lambda b, pt: (b, 0, 0))
    win_spec = pl.BlockSpec((None, None, NSA_KV_HEADS, HEAD_DIM, n_win), lambda b, pt: (layer, b, 0, 0, 0))

    def page_spec(p):
        return pl.BlockSpec((None, None, NSA_KV_HEADS, HEAD_DIM, PAGE_SIZE),
                            lambda b, pt: (layer, pt[b * n_pages + p], 0, 0, 0))

    pages = [page_spec(p) for p in range(n_pages)]
    return pl.pallas_call(
        functools.partial(_nsa_dec_kernel, n_pages, past),
        grid_spec=pltpu.PrefetchScalarGridSpec(
            num_scalar_prefetch=1,
            grid=(db,),
            in_specs=[row] * 7 + [cmp_spec, cmp_spec] + [row] * 4 + [win_spec, win_spec]
            + [pl.BlockSpec(pair_mat.shape, lambda b, pt: (0, 0))] + pages + pages,
            out_specs=[row] * 3,
            scratch_shapes=[pltpu.VMEM((n_sel_pad, LANES), F32),
                            pltpu.VMEM((LANES, KV_WIDTH), F32), pltpu.VMEM((LANES, KV_WIDTH), F32)]),
        out_shape=[jax.ShapeDtypeStruct((m, KV_WIDTH), F32)] * 3,
        compiler_params=_cp("parallel"),
        name="nsa_decode",
    )(pt, *qn, *qr, gates, kc, vc, *new_rows, win_kt, win_vt, pair_mat,
      *([pool_kt] * n_pages), *([pool_vt] * n_pages))


NSA_IN = SELF_WIDTH + 6 * KV_WIDTH + 3 * N_SELF_HEADS + MEM_WIDTH
ROW_TILE = 512
SB_PAGES_PER_STEP = 8


def _nsa_head_order():
    return [NSA_GROUP * g + r for r in range(NSA_GROUP) for g in range(NSA_KV_HEADS)]


def _nsa_in_columns():
    q = [h * HEAD_DIM + d for h in _nsa_head_order() for d in range(HEAD_DIM)]
    kv = list(range(SELF_WIDTH, SELF_WIDTH + 6 * KV_WIDTH))
    g0 = SELF_WIDTH + 6 * KV_WIDTH
    gate = []
    for gp in range(KV_WIDTH // LANES):
        for c in range(LANES):
            if c < 3 * 2 * NSA_GROUP:
                branch, r, hf = c // (2 * NSA_GROUP), (c % (2 * NSA_GROUP)) // 2, c % 2
                gate.append(g0 + (NSA_GROUP * (2 * gp + hf) + r) * 3 + branch)
            else:
                gate.append(NSA_IN)
    qm = list(range(g0 + 3 * N_SELF_HEADS, NSA_IN))
    cols = np.asarray(q + kv + gate + qm, np.int32)
    assert cols.shape[0] == _NSA_COLS
    return cols


def _nsa_out_rows():
    o = [h * HEAD_DIM + d for h in _nsa_head_order() for d in range(HEAD_DIM)]
    return np.asarray(o + list(range(SELF_WIDTH, SELF_WIDTH + MEM_WIDTH)), np.int32)


def _block_diag2(w):
    eye = jnp.eye(LANES // HEAD_DIM, dtype=w.dtype)
    return jnp.einsum("gh,lde->lgdhe", eye, w).reshape(w.shape[0], LANES, LANES).astype(BF)


def kernel(x_prompt, x_sample, mem_prompt, cache_sb_k, cache_sb_v, cache_nsa_cmp_k, cache_nsa_cmp_v, cache_nsa_sel_k, cache_nsa_sel_v, cache_nsa_win_k, cache_nsa_win_v, cache_mem_k, cache_mem_v, page_table, ln_mix_pre, ln_mix_post, ln_ffn_pre, ln_ffn_post, ln_mem, w_in_a, w_in_b, w_cmp_k, w_cmp_v, pe_cmp_k, pe_cmp_v, w_out, w_mem_kv, w_gate_up, w_down):
    bsz, seq, d = x_prompt.shape
    db, ds, _ = x_sample.shape
    n_pages = page_table.shape[1]
    past = n_pages * PAGE_SIZE
    depth = w_out.shape[0]
    n_win = cache_nsa_win_k.shape[2]
    tm_p = min(ROW_TILE, bsz * seq)
    tm_s = min(ROW_TILE, db * ds)
    assert d == D_MODEL and seq % KEY_BLOCK == 0 and seq % tm_p == 0 and (db * ds) % tm_s == 0 and tm_s % ds == 0
    assert ds & (ds - 1) == 0 and N_SELF_HEADS * ds <= LANES and ds < CMP_BLOCK
    assert past % SEL_BLOCK == 0 and past + ds <= past + SEL_BLOCK and n_win % LANES == 0 and n_win <= past
    assert n_pages % SB_PAGES_PER_STEP == 0 and seq >= WINDOW

    h_p = x_prompt.reshape(bsz * seq, d)
    h_s = x_sample.reshape(db * ds, d)
    mem = mem_prompt.reshape(bsz * N_MEM, d)
    pt = page_table.reshape(-1).astype(jnp.int32)
    u256 = jnp.triu(jnp.ones((KEY_BLOCK, KEY_BLOCK), BF), 1)
    ut128 = jnp.tril(jnp.ones((LANES, LANES), BF), -1)
    pair_np = np.zeros((past // CMP_BLOCK, LANES), np.float32)
    pair_np[np.arange(past // CMP_BLOCK), np.arange(past // CMP_BLOCK) // (SEL_BLOCK // CMP_BLOCK)] = 1.0
    pair_mat = jnp.asarray(pair_np, BF)
    tmin = lambda c: jnp.transpose(c, (0, 1, 3, 4, 2))
    sbk_t, sbv_t = tmin(cache_sb_k), tmin(cache_sb_v)
    cmpk_t, cmpv_t = tmin(cache_nsa_cmp_k), tmin(cache_nsa_cmp_v)
    selk_t, selv_t = tmin(cache_nsa_sel_k), tmin(cache_nsa_sel_v)
    wink_t, winv_t = tmin(cache_nsa_win_k), tmin(cache_nsa_win_v)
    memk_t, memv_t = tmin(cache_mem_k), tmin(cache_mem_v)
    rope_p = _rope_tables(jnp.arange(seq, dtype=jnp.int32))
    rope_s = _rope_tables(past + jnp.arange(tm_s, dtype=jnp.int32) % ds)

    nsa_cols = _nsa_in_columns()
    nsa_rows = _nsa_out_rows()

    sb_p, sb_s = [[], []], [[], []]
    nsa_p, nsa_s = [[] for _ in range(6)], [[] for _ in range(6)]
    memk_p, memv_p = [], []
    for i in range(depth):
        j = i // 2
        g_pre = ln_mix_pre[i][None]
        tail_w = (ln_mix_post[i][None], ln_ffn_pre[i][None], ln_ffn_post[i][None],
                  w_gate_up[i].astype(BF), w_down[i].astype(BF))
        wkv = w_mem_kv[i]
        mk_bf, mkt, mvt, mvt_bf = _memkv(mem, ln_mem[i][None], wkv.astype(BF), wkv.T.astype(BF))
        memk_p.append(mkt)
        memv_p.append(mvt)
        if i % 2 == 0:
            w = w_in_a[j]
            w_bf = w.astype(BF)
            wt = w[:, SELF_WIDTH:3 * SELF_WIDTH].T.astype(BF)
            q, qm, kb, kt, vt, vt3 = _sb_inproj(h_p, g_pre, w_bf, wt, tm=tm_p, seq=seq)
            o = _sb_attn(q, kb, vt3, u256, batch=bsz, seq=seq)
            om = _mem_attn(qm, mk_bf, mvt_bf, batch=bsz, seq=seq)
            qs, qms, ks, vs = _sb_inproj(h_s, g_pre, w_bf, wt, tm=tm_s, seq=None)
            o_s = _sb_decode(pt, qs, ks, vs, sbk_t, sbv_t, ut128, layer=j, n_pages=n_pages,
                             pps=SB_PAGES_PER_STEP)
            om_s = _mem_decode(qms, memk_t, memv_t, layer=i, t=ds)
            wo = w_out[i].astype(BF)
            mix_p, mix_s = [o, om], [o_s, om_s]
            for n, (rows_t, rows) in enumerate(((kt, ks), (vt, vs))):
                sb_p[n].append(rows_t)
                sb_s[n].append(rows.reshape(db, ds, N_SELF_HEADS, HEAD_DIM))
        else:
            w = w_in_b[j]
            w_bf = jnp.concatenate([w, jnp.zeros((d, 1), w.dtype)], axis=1)[:, nsa_cols].astype(BF)
            kv_cols = lambda n: w[:, SELF_WIDTH + n * KV_WIDTH:SELF_WIDTH + (n + 1) * KV_WIDTH]
            wt = jnp.concatenate([kv_cols(0), kv_cols(1), kv_cols(3), kv_cols(5)], axis=1).T.astype(BF)
            pek2 = jnp.tile(pe_cmp_k[j], (1, LANES // HEAD_DIM))
            pev2 = jnp.tile(pe_cmp_v[j], (1, LANES // HEAD_DIM))
            wck2 = _block_diag2(w_cmp_k[j])
            wcv2 = _block_diag2(w_cmp_v[j])
            outs = _nsa_inproj(h_p, g_pre, w_bf, wt, *rope_p, tm=tm_p, seq=seq)
            qn, qr, gates, qm = outs[0:3], outs[3:6], outs[6], outs[7]
            kc, vc, ksb, kwb = outs[8:12]
            rows_t, (vst3, vwt3) = outs[12:18], outs[18:20]
            kc_bf, _ = _compress(kc, pek2, wck2, rows=seq)
            _, vct_bf = _compress(vc, pev2, wcv2, rows=seq)
            o3 = _nsa_attn(qn, qr, kc_bf, vct_bf, ksb, vst3, kwb, vwt3, gates, batch=bsz, seq=seq)
            om = _mem_attn(qm, mk_bf, mvt_bf, batch=bsz, seq=seq)
            outs = _nsa_inproj(h_s, g_pre, w_bf, wt, *rope_s, tm=tm_s, seq=None)
            qn_s, qr_s, gates_s, qms, rows_s = outs[0:3], outs[3:6], outs[6], outs[7], outs[8:14]
            kc_s, vc_s = _cmp_decode(pt, cmpk_t, cmpv_t, pek2, pev2, wck2, wcv2, layer=j, n_pages=n_pages, db=db)
            o3_s = _nsa_decode(pt, qn_s, qr_s, gates_s, kc_s, vc_s, rows_s[2:6], wink_t, winv_t, selk_t, selv_t,
                               pair_mat, layer=j, n_pages=n_pages, t=ds)
            om_s = _mem_decode(qms, memk_t, memv_t, layer=i, t=ds)
            wo = w_out[i][nsa_rows].astype(BF)
            mix_p, mix_s = [*o3, om], [*o3_s, om_s]
            for n in range(6):
                nsa_p[n].append(rows_t[n][:, :, seq - min(WINDOW, seq):] if n >= 4 else rows_t[n])
                nsa_s[n].append(rows_s[n].reshape(db, ds, NSA_KV_HEADS, HEAD_DIM))
        h_p = _layer_tail(h_p, mix_p, wo, *tail_w, tm=tm_p)
        h_s = _layer_tail(h_s, mix_s, wo, *tail_w, tm=tm_s)

    st = lambda rows: jnp.stack(rows, axis=0)

    def tokens_first(rows_t):
        x = st(rows_t)
        nl, nb, width, nt = x.shape
        return x.reshape(nl, nb, width // HEAD_DIM, HEAD_DIM, nt).transpose(0, 1, 4, 2, 3)

    return (h_p.reshape(bsz, seq, d), h_s.reshape(db, ds, d),
            tokens_first(sb_p[0]), tokens_first(sb_p[1]),
            *[tokens_first(x) for x in nsa_p],
            tokens_first(memk_p), tokens_first(memv_p),
            st(sb_s[0]), st(sb_s[1]),
            *[st(x) for x in nsa_s])
```

```python
import functools
import math

import jax
import jax.numpy as jnp
import numpy as np
from jax import lax
from jax.experimental import pallas as pl
from jax.experimental.pallas import tpu as pltpu

F32 = jnp.float32
BF = jnp.bfloat16

D_MODEL = 1024
HEAD_DIM = 64
N_SELF_HEADS = 12
N_MEM_HEADS = 4
SELF_WIDTH = N_SELF_HEADS * HEAD_DIM
MEM_WIDTH = N_MEM_HEADS * HEAD_DIM
N_MEM = 256
NSA_KV_HEADS = 4
NSA_GROUP = N_SELF_HEADS // NSA_KV_HEADS
KV_WIDTH = NSA_KV_HEADS * HEAD_DIM
CMP_BLOCK = 32
SEL_BLOCK = 64
SEL_TOPK = 16
WINDOW = 512
PAGE_SIZE = 128
ROPE_THETA = 500000.0
ROT_DIM = HEAD_DIM // 4
EPS = 1e-6
NEG = -1e30
SCALE = HEAD_DIM ** -0.5
Q_SCALE = SCALE * 1.4426950408889634

LANES = 128
KEY_BLOCK = 256
VMEM_LIMIT = 56 * 1024 * 1024


def _cp(*sem):
    return pltpu.CompilerParams(dimension_semantics=sem, vmem_limit_bytes=VMEM_LIMIT)


def _dot(a, b):
    return jnp.dot(a, b, preferred_element_type=F32)


def _dot_nt(a, b):
    return lax.dot_general(a, b, (((1,), (1,)), ((), ())), preferred_element_type=F32)


def _dot_tn(a, b):
    return lax.dot_general(a, b, (((0,), (0,)), ((), ())), preferred_element_type=F32)


def _rms(x, g):
    return x * lax.rsqrt(jnp.mean(x * x, axis=-1, keepdims=True) + EPS) * g


def _sigmoid(x):
    return 1.0 / (1.0 + jnp.exp(-x))


def _neg_abs(y):
    bits = lax.bitcast_convert_type(y, jnp.int32) | jnp.int32(-2 ** 31)
    return lax.bitcast_convert_type(bits, F32)


def _sb_log_keep(y):
    return jnp.minimum(y, 0.0) - jnp.log2(1.0 + jnp.exp2(_neg_abs(y)))


def _head_of(idx):
    return jnp.right_shift(idx, 6)


def _const_spec(shape):
    nd = len(shape)
    return pl.BlockSpec(shape, lambda *_: (0,) * nd, pipeline_mode=pl.Buffered(1))


def _rope128(x, c, s):
    lane = lax.broadcasted_iota(jnp.int32, x.shape, 1)
    up = pltpu.roll(x, LANES - ROT_DIM // 2, axis=1)
    dn = pltpu.roll(x, ROT_DIM // 2, axis=1)
    sw = jnp.where(jnp.bitwise_and(lane, HEAD_DIM - 1) < ROT_DIM // 2, up, dn)
    return x * c + sw * s


def _rope_tables(pos):
    half = ROT_DIM // 2
    inv = ROPE_THETA ** (-jnp.arange(half, dtype=F32) * 2.0 / ROT_DIM)
    ang = pos.astype(F32)[:, None] * inv[None, :]
    cos, sin = jnp.cos(ang), jnp.sin(ang)
    n = pos.shape[0]
    c64 = jnp.concatenate([cos, cos, jnp.ones((n, HEAD_DIM - ROT_DIM), F32)], axis=1)
    s64 = jnp.concatenate([-sin, sin, jnp.zeros((n, HEAD_DIM - ROT_DIM), F32)], axis=1)
    return jnp.tile(c64, (1, LANES // HEAD_DIM)), jnp.tile(s64, (1, LANES // HEAD_DIM))


def _store_vt3(vt3_ref, vt):
    for c in range(vt3_ref.shape[0]):
        vt3_ref[c] = vt[:, c * KEY_BLOCK:(c + 1) * KEY_BLOCK].astype(BF)


def _sb_inproj_kernel(prompt, x_ref, g_ref, w_ref, wt_ref, q_ref, qm_ref, *outs):
    xn = _rms(x_ref[...], g_ref[...]).astype(BF)
    q_ref[...] = (_dot(xn, w_ref[:, 0:SELF_WIDTH]) * -Q_SCALE).astype(q_ref.dtype)
    qm_ref[...] = (_dot(xn, w_ref[:, 3 * SELF_WIDTH:3 * SELF_WIDTH + MEM_WIDTH]) * Q_SCALE).astype(qm_ref.dtype)
    k = _dot(xn, w_ref[:, SELF_WIDTH:2 * SELF_WIDTH])
    if prompt:
        kb_ref, kt_ref, vt_ref, vt3_ref = outs
        kb_ref[...] = k.astype(BF)
        kt_ref[...] = _dot_nt(wt_ref[0:SELF_WIDTH, :], xn)
        vt = _dot_nt(wt_ref[SELF_WIDTH:2 * SELF_WIDTH, :], xn)
        vt_ref[...] = vt
        _store_vt3(vt3_ref, vt)
    else:
        k_ref, v_ref = outs
        k_ref[...] = k
        v_ref[...] = _dot(xn, w_ref[:, 2 * SELF_WIDTH:3 * SELF_WIDTH])


def _sb_inproj(x, g, w_bf, wt_bf, *, tm, seq):
    m = x.shape[0]
    prompt = seq is not None
    row = lambda i: (i, 0)
    qdt = BF if prompt else F32
    out_shape = [jax.ShapeDtypeStruct((m, SELF_WIDTH), qdt), jax.ShapeDtypeStruct((m, MEM_WIDTH), qdt)]
    out_specs = [pl.BlockSpec((tm, SELF_WIDTH), row), pl.BlockSpec((tm, MEM_WIDTH), row)]
    if prompt:
        per_seq = seq // tm
        tmaj = lambda i: (i // per_seq, 0, i % per_seq)
        out_shape += [jax.ShapeDtypeStruct((m, SELF_WIDTH), BF)]
        out_shape += [jax.ShapeDtypeStruct((m // seq, SELF_WIDTH, seq), F32)] * 2
        out_shape += [jax.ShapeDtypeStruct((m // KEY_BLOCK, SELF_WIDTH, KEY_BLOCK), BF)]
        out_specs += [pl.BlockSpec((tm, SELF_WIDTH), row)] + [pl.BlockSpec((None, SELF_WIDTH, tm), tmaj)] * 2
        out_specs += [pl.BlockSpec((tm // KEY_BLOCK, SELF_WIDTH, KEY_BLOCK), lambda i: (i, 0, 0))]
    else:
        out_shape += [jax.ShapeDtypeStruct((m, SELF_WIDTH), F32)] * 2
        out_specs += [pl.BlockSpec((tm, SELF_WIDTH), row)] * 2
    return pl.pallas_call(
        functools.partial(_sb_inproj_kernel, prompt),
        grid=(m // tm,),
        in_specs=[pl.BlockSpec((tm, D_MODEL), row), _const_spec((1, D_MODEL)),
                  _const_spec(w_bf.shape), _const_spec(wt_bf.shape)],
        out_specs=out_specs,
        out_shape=out_shape,
        compiler_params=_cp("parallel"),
        name="sb_inproj",
    )(x, g, w_bf, wt_bf)


_NSA_Q0 = 0
_NSA_KV0 = SELF_WIDTH
_NSA_G0 = _NSA_KV0 + 6 * KV_WIDTH
_NSA_QM0 = _NSA_G0 + 2 * LANES
_NSA_COLS = _NSA_QM0 + MEM_WIDTH


def _nsa_inproj_kernel(prompt, x_ref, g_ref, w_ref, wt_ref, c_ref, s_ref,
                       qn0, qn1, qn2, qr0, qr1, qr2, gate_ref, qm_ref, *outs):
    xn = _rms(x_ref[...], g_ref[...]).astype(BF)
    c = c_ref[...]
    s = s_ref[...]

    def rope256(y):
        return jnp.concatenate([_rope128(y[:, :LANES], c, s), _rope128(y[:, LANES:], c, s)], axis=1)

    for r, (qn_ref, qr_ref) in enumerate(((qn0, qr0), (qn1, qr1), (qn2, qr2))):
        q = _dot(xn, w_ref[:, r * KV_WIDTH:(r + 1) * KV_WIDTH]) * Q_SCALE
        qn_ref[...] = q.astype(qn_ref.dtype)
        qr_ref[...] = rope256(q).astype(qr_ref.dtype)
    gate_ref[...] = _dot(xn, w_ref[:, _NSA_G0:_NSA_G0 + 2 * LANES])
    qm_ref[...] = (_dot(xn, w_ref[:, _NSA_QM0:_NSA_QM0 + MEM_WIDTH]) * Q_SCALE).astype(qm_ref.dtype)

    def kv(j):
        return _dot(xn, w_ref[:, _NSA_KV0 + j * KV_WIDTH:_NSA_KV0 + (j + 1) * KV_WIDTH])

    def kvt(j):
        return _dot_nt(wt_ref[j * KV_WIDTH:(j + 1) * KV_WIDTH, :], xn)

    ks = rope256(kv(2))
    kw = rope256(kv(4))
    if prompt:
        kc_ref, vc_ref, ksb_ref, kwb_ref, kct_ref, vct_ref, kst_ref, vst_ref, kwt_ref, vwt_ref, vs3_ref, vw3_ref = outs
        kc_ref[...] = kv(0)
        vc_ref[...] = kv(1)
        ksb_ref[...] = ks.astype(BF)
        kwb_ref[...] = kw.astype(BF)
        kct_ref[...] = kvt(0)
        vct_ref[...] = kvt(1)
        kst_ref[...] = ks.T
        kwt_ref[...] = kw.T
        vst = kvt(2)
        vst_ref[...] = vst
        _store_vt3(vs3_ref, vst)
        vwt = kvt(3)
        vwt_ref[...] = vwt
        _store_vt3(vw3_ref, vwt)
    else:
        kc_ref, vc_ref, ks_ref, vs_ref, kw_ref, vw_ref = outs
        kc_ref[...] = kv(0)
        vc_ref[...] = kv(1)
        ks_ref[...] = ks
        vs_ref[...] = kv(3)
        kw_ref[...] = kw
        vw_ref[...] = kv(5)


def _nsa_inproj(x, g, w_bf, wt_bf, ctab, stab, *, tm, seq):
    m = x.shape[0]
    prompt = seq is not None
    row = lambda i: (i, 0)
    tbl_blocks = ctab.shape[0] // tm
    tbl = lambda i: (i % tbl_blocks, 0)
    qdt = BF if prompt else F32
    kvs = lambda dt: jax.ShapeDtypeStruct((m, KV_WIDTH), dt)
    rows = pl.BlockSpec((tm, KV_WIDTH), row)
    out_shape = [kvs(qdt)] * 6 + [kvs(F32), kvs(qdt)]
    out_specs = [rows] * 8
    if prompt:
        per_seq = seq // tm
        tmaj = lambda i: (i // per_seq, 0, i % per_seq)
        out_shape += [kvs(F32)] * 2 + [kvs(BF)] * 2 + [jax.ShapeDtypeStruct((m // seq, KV_WIDTH, seq), F32)] * 6
        out_shape += [jax.ShapeDtypeStruct((m // KEY_BLOCK, KV_WIDTH, KEY_BLOCK), BF)] * 2
        out_specs += [rows] * 4 + [pl.BlockSpec((None, KV_WIDTH, tm), tmaj)] * 6
        out_specs += [pl.BlockSpec((tm // KEY_BLOCK, KV_WIDTH, KEY_BLOCK), lambda i: (i, 0, 0))] * 2
    else:
        out_shape += [kvs(F32)] * 6
        out_specs += [rows] * 6
    return pl.pallas_call(
        functools.partial(_nsa_inproj_kernel, prompt),
        grid=(m // tm,),
        in_specs=[pl.BlockSpec((tm, D_MODEL), row), _const_spec((1, D_MODEL)),
                  _const_spec(w_bf.shape), _const_spec(wt_bf.shape),
                  pl.BlockSpec((tm, LANES), tbl), pl.BlockSpec((tm, LANES), tbl)],
        out_specs=out_specs,
        out_shape=out_shape,
        compiler_params=_cp("parallel"),
        name="nsa_inproj",
    )(x, g, w_bf, wt_bf, ctab, stab)


def _memkv_kernel(x_ref, g_ref, w_ref, wt_ref, kb_ref, kt_ref, vt_ref, vtb_ref):
    xn = _rms(x_ref[...], g_ref[...]).astype(BF)
    kb_ref[...] = _dot(xn, w_ref[:, :MEM_WIDTH]).astype(BF)
    kt_ref[...] = _dot_nt(wt_ref[0:MEM_WIDTH, :], xn)
    vt = _dot_nt(wt_ref[MEM_WIDTH:, :], xn)
    vt_ref[...] = vt
    vtb_ref[...] = vt.astype(BF)


def _memkv(x, g, w_bf, wt_bf):
    m = x.shape[0]
    row = lambda i: (i, 0)
    tmaj = pl.BlockSpec((None, MEM_WIDTH, N_MEM), lambda i: (i, 0, 0))
    return pl.pallas_call(
        _memkv_kernel,
        grid=(m // N_MEM,),
        in_specs=[pl.BlockSpec((N_MEM, D_MODEL), row), _const_spec((1, D_MODEL)),
                  _const_spec(w_bf.shape), _const_spec(wt_bf.shape)],
        out_specs=[pl.BlockSpec((N_MEM, MEM_WIDTH), row), tmaj, tmaj, pl.BlockSpec((MEM_WIDTH, N_MEM), lambda i: (0, i))],
        out_shape=[jax.ShapeDtypeStruct((m, MEM_WIDTH), BF)]
        + [jax.ShapeDtypeStruct((m // N_MEM, MEM_WIDTH, N_MEM), F32)] * 2 + [jax.ShapeDtypeStruct((MEM_WIDTH, m), BF)],
        compiler_params=_cp("parallel"),
        name="memkv",
    )(x, g, w_bf, wt_bf)


def _tail_kernel(n_mix, *refs):
    h_ref = refs[0]
    mix_refs = refs[1:1 + n_mix]
    wo_ref, gpost_ref, gpre_ref, gfpost_ref, wgu_ref, wd_ref, o_ref = refs[1 + n_mix:]
    y = None
    off = 0
    for mref in mix_refs:
        w = mref.shape[1]
        part = _dot(mref[...].astype(BF), wo_ref[off:off + w, :])
        y = part if y is None else y + part
        off += w
    h = h_ref[...] + _rms(y, gpost_ref[...])
    xn = _rms(h, gpre_ref[...]).astype(BF)
    d_ff = wd_ref.shape[0]
    acc = None
    for c0 in range(0, d_ff, KEY_BLOCK):
        gt = _dot(xn, wgu_ref[:, c0:c0 + KEY_BLOCK])
        up = _dot(xn, wgu_ref[:, d_ff + c0:d_ff + c0 + KEY_BLOCK])
        act = (gt * _sigmoid(gt) * up).astype(BF)
        part = _dot(act, wd_ref[c0:c0 + KEY_BLOCK, :])
        acc = part if acc is None else acc + part
    o_ref[...] = h + _rms(acc, gfpost_ref[...])


def _layer_tail(h, mixes, wo_bf, g_post, g_pre, g_fpost, wgu_bf, wd_bf, *, tm):
    m = h.shape[0]
    row = lambda i: (i, 0)
    const = lambda i: (0, 0)
    return pl.pallas_call(
        functools.partial(_tail_kernel, len(mixes)),
        grid=(m // tm,),
        in_specs=[pl.BlockSpec((tm, D_MODEL), row)] + [pl.BlockSpec((tm, x.shape[1]), row) for x in mixes]
        + [_const_spec(wo_bf.shape)] + [_const_spec((1, D_MODEL))] * 3
        + [_const_spec(wgu_bf.shape), _const_spec(wd_bf.shape)],
        out_specs=pl.BlockSpec((tm, D_MODEL), row),
        out_shape=jax.ShapeDtypeStruct((m, D_MODEL), F32),
        compiler_params=_cp("parallel"),
        name="layer_tail",
    )(h, *mixes, wo_bf, g_post, g_pre, g_fpost, wgu_bf, wd_bf)


def _half_masks(shape, axis):
    idx = lax.broadcasted_iota(jnp.int32, shape, axis)
    return idx < HEAD_DIM, idx >= HEAD_DIM


def _sb_attn_kernel(q_ref, k_ref, vt_ref, u_ref, o_ref, acc_ref, carry_ref):
    tq = q_ref.shape[0]
    i = pl.program_id(2)
    qi = q_ref[...]
    lo_l, hi_l = _half_masks(qi.shape, 1)
    zero = jnp.zeros_like(qi)
    qh = (jnp.where(lo_l, qi, zero), jnp.where(hi_l, qi, zero))
    acc_ref[...] = jnp.zeros_like(acc_ref)
    carry_ref[...] = jnp.zeros_like(carry_ref)
    u = u_ref[...]

    def run(blocks):
        carry = [carry_ref[h:h + 1, :] for h in range(2)]
        staged = []
        for j, diag in blocks:
            kb = k_ref[pl.ds(pl.multiple_of(j * tq, tq), tq), :]
            vt = vt_ref[j]
            lo_r, hi_r = _half_masks(vt.shape, 0)
            zv = jnp.zeros_like(vt)
            vth = (jnp.where(lo_r, vt, zv), jnp.where(hi_r, vt, zv))
            mask = None
            if diag:
                kr = lax.broadcasted_iota(jnp.int32, (tq, tq), 0)
                qc = lax.broadcasted_iota(jnp.int32, (tq, tq), 1)
                mask = kr < qc
            for h in range(2):
                y = _dot_nt(kb, qh[h])
                lk = _sb_log_keep(y)
                if diag:
                    lk = jnp.where(mask, lk, 0.0)
                after = _dot(u, lk.astype(BF))
                staged.append((h, y, lk, after, vth[h], mask))
        contrib = None
        for h, y, lk, after, vth_h, mask in staged:
            a = jnp.exp2((lk - y) + after + carry[h])
            if mask is not None:
                a = jnp.where(mask, a, 0.0)
            carry[h] = carry[h] + after[0:1, :] + lk[0:1, :]
            part = _dot(vth_h, a.astype(BF))
            contrib = part if contrib is None else contrib + part
        acc_ref[...] += contrib
        for h in range(2):
            carry_ref[h:h + 1, :] = carry[h]

    run([(i, True)])
    one = jnp.bitwise_and(i, 1)
    two = jnp.bitwise_and(i, 2)

    @pl.when(one == 1)
    def _():
        run([(i - 1, False)])

    @pl.when(two == 2)
    def _():
        run([(i - 1 - one, False), (i - 2 - one, False)])

    def body(jj, c):
        j = i - 1 - one - two - 4 * jj
        run([(j - u, False) for u in range(4)])
        return c

    lax.fori_loop(0, jnp.right_shift(i, 2), body, 0)
    o_ref[...] = acc_ref[...].T.astype(o_ref.dtype)


def _sb_attn(q_bf, k_bf, vt3, u_bf, *, batch, seq):
    m = q_bf.shape[0]
    tq = KEY_BLOCK
    nq = seq // tq
    return pl.pallas_call(
        _sb_attn_kernel,
        grid=(batch, SELF_WIDTH // LANES, nq),
        in_specs=[pl.BlockSpec((tq, LANES), lambda b, p, i: (b * nq + i, p)),
                  pl.BlockSpec((seq, LANES), lambda b, p, i: (b, p)),
                  pl.BlockSpec((nq, LANES, tq), lambda b, p, i: (b, p, 0)),
                  _const_spec((tq, tq))],
        out_specs=pl.BlockSpec((tq, LANES), lambda b, p, i: (b * nq + i, p)),
        out_shape=jax.ShapeDtypeStruct((m, SELF_WIDTH), BF),
        scratch_shapes=[pltpu.VMEM((LANES, tq), F32), pltpu.VMEM((8, tq), F32)],
        compiler_params=_cp("parallel", "parallel", "arbitrary"),
        name="sb_attn",
    )(q_bf, k_bf, vt3, u_bf)


def _mem_attn_kernel(q_ref, k_ref, vt_ref, o_ref):
    qi = q_ref[...]
    k = k_ref[...]
    vt = vt_ref[...]
    lane_h = _head_of(lax.broadcasted_iota(jnp.int32, qi.shape, 1))
    row_h = _head_of(lax.broadcasted_iota(jnp.int32, vt.shape, 0))
    acc = None
    for h in range(N_MEM_HEADS):
        s = _dot_nt(k, jnp.where(lane_h == h, qi, jnp.zeros_like(qi)))
        e = jnp.exp2(s - jnp.max(s, axis=0, keepdims=True))
        p = e / jnp.sum(e, axis=0, keepdims=True)
        part = _dot(jnp.where(row_h == h, vt, jnp.zeros_like(vt)), p.astype(BF))
        acc = part if acc is None else acc + part
    o_ref[...] = acc.T.astype(o_ref.dtype)


def _mem_attn(qm_bf, mk_bf, mvt_bf, *, batch, seq):
    m = qm_bf.shape[0]
    tq = KEY_BLOCK
    nq = seq // tq
    return pl.pallas_call(
        _mem_attn_kernel,
        grid=(batch, nq),
        in_specs=[pl.BlockSpec((tq, MEM_WIDTH), lambda b, i: (b * nq + i, 0)),
                  pl.BlockSpec((N_MEM, MEM_WIDTH), lambda b, i: (b, 0)),
                  pl.BlockSpec((MEM_WIDTH, N_MEM), lambda b, i: (0, b))],
        out_specs=pl.BlockSpec((tq, MEM_WIDTH), lambda b, i: (b * nq + i, 0)),
        out_shape=jax.ShapeDtypeStruct((m, MEM_WIDTH), BF),
        compiler_params=_cp("parallel", "parallel"),
        name="mem_attn",
    )(qm_bf, mk_bf, mvt_bf)


def _compress_rows(xa_ref, xb_ref, pe_ref, w_ref, nblk):
    acc = None
    for l in range(CMP_BLOCK):
        xs = [ref[pl.ds(l, nblk, stride=CMP_BLOCK), :] for ref in (xa_ref, xb_ref)]
        xl = jnp.concatenate(xs, axis=0) + pe_ref[l:l + 1, :]
        part = _dot(xl.astype(BF), w_ref[l])
        acc = part if acc is None else acc + part
    return jnp.concatenate([acc[:nblk], acc[nblk:]], axis=1)


def _compress_kernel(xa_ref, xb_ref, pe_ref, w_ref, o_ref, ot_ref):
    out = _compress_rows(xa_ref, xb_ref, pe_ref, w_ref, o_ref.shape[0])
    o_ref[...] = out.astype(o_ref.dtype)
    ot_ref[...] = out.T.astype(ot_ref.dtype)


def _compress(x, pe2, wbd2, *, rows):
    m = x.shape[0]
    nblk = rows // CMP_BLOCK
    return pl.pallas_call(
        _compress_kernel,
        grid=(m // rows,),
        in_specs=[pl.BlockSpec((rows, LANES), lambda i: (i, 0)), pl.BlockSpec((rows, LANES), lambda i: (i, 1)),
                  _const_spec(pe2.shape), _const_spec(wbd2.shape)],
        out_specs=[pl.BlockSpec((nblk, KV_WIDTH), lambda i: (i, 0)), pl.BlockSpec((KV_WIDTH, nblk), lambda i: (0, i))],
        out_shape=[jax.ShapeDtypeStruct((m // CMP_BLOCK, KV_WIDTH), BF),
                   jax.ShapeDtypeStruct((KV_WIDTH, m // CMP_BLOCK), BF)],
        compiler_params=_cp("parallel"),
        name="compress",
    )(x, x, pe2, wbd2)


def _topk_select(score_ref, n_blocks, cur):
    score = score_ref[...]
    blk = lax.broadcasted_iota(jnp.int32, score.shape, 0)

    def body(i, cnt):
        si = score_ref[pl.ds(i, 1), :]
        tie = jnp.where(blk > i, 1.0, 0.0)
        return cnt + jnp.where(si > score, 1.0, jnp.where(si == score, tie, 0.0))

    rank = lax.fori_loop(0, n_blocks, body, jnp.zeros(score.shape, F32))
    return jnp.where((rank < SEL_TOPK) & (blk <= cur), 1.0, 0.0).astype(F32)


def _nsa_attn_kernel(qn0, qn1, qn2, qr0, qr1, qr2, kc_ref, vct_ref, ks_ref, vst_ref, kw_ref, vwt_ref, gate_ref,
                     o0, o1, o2, imp_ref, score_ref, sel_ref, acc_ref, occ_ref):
    tq = qn0.shape[0]
    tk = tq
    i = pl.program_id(2)
    qn_refs = (qn0, qn1, qn2)
    qr_refs = (qr0, qr1, qr2)
    o_refs = (o0, o1, o2)
    qpos = i * tq + lax.broadcasted_iota(jnp.int32, (1, tq), 1)

    def halves(x):
        lo, hi = _half_masks(x.shape, 1)
        z = jnp.zeros_like(x)
        return jnp.where(lo, x, z), jnp.where(hi, x, z)

    def by_row_half(a, b):
        lo, _ = _half_masks((LANES, tq), 0)
        return jnp.where(lo, a, b)

    def vhalves(vt):
        lo, hi = _half_masks(vt.shape, 0)
        z = jnp.zeros_like(vt)
        return jnp.where(lo, vt, z), jnp.where(hi, vt, z)

    n_cmp = kc_ref.shape[0]
    kc = kc_ref[...]
    vct = vhalves(vct_ref[...])
    nrow = lax.broadcasted_iota(jnp.int32, (n_cmp, tq), 0)
    c_mask = (nrow * CMP_BLOCK + (CMP_BLOCK - 1)) <= qpos
    imp = [None, None]
    for r in range(NSA_GROUP):
        qh = halves(qn_refs[r][...])
        occ = None
        for hf in range(2):
            s = jnp.where(c_mask, _dot_nt(kc, qh[hf]), NEG)
            e = jnp.exp2(s - jnp.max(s, axis=0, keepdims=True))
            p = jnp.where(c_mask, e / jnp.sum(e, axis=0, keepdims=True), 0.0)
            imp[hf] = p if imp[hf] is None else imp[hf] + p
            part = _dot(vct[hf], p.astype(BF))
            occ = part if occ is None else occ + part
        occ_ref[r] = occ

    n_sel = sel_ref.shape[1]
    ratio = SEL_BLOCK // CMP_BLOCK
    blk = lax.broadcasted_iota(jnp.int32, (n_sel, tq), 0)
    cur = _head_of(qpos)
    forced = (blk == 0) | (blk == cur) | (blk == cur - 1)
    allowed = blk <= cur
    scores = []
    for hf in range(2):
        parts = []
        for c in range(tq // LANES):
            imp_ref[c] = imp[hf][:, c * LANES:(c + 1) * LANES]
            parts.append(imp_ref[c, pl.ds(0, n_sel, stride=ratio), :] + imp_ref[c, pl.ds(1, n_sel, stride=ratio), :])
        blk_imp = jnp.concatenate(parts, axis=1)
        scores.append(jnp.where(forced, jnp.inf, jnp.where(allowed, blk_imp, -jnp.inf)))
    for hf in range(2):
        score_ref[...] = scores[hf]
        sel_ref[hf] = _topk_select(score_ref, jnp.minimum(n_sel, (i + 1) * (tq // SEL_BLOCK)), cur)

    qr_h = [halves(qr_refs[r][...]) for r in range(NSA_GROUP)]
    krow = lax.broadcasted_iota(jnp.int32, (tk, tq), 0)
    n_sub = tk // SEL_BLOCK
    n_heads = 2 * NSA_GROUP

    def attend(k_ref, vt_ref, mask_of, j_lo):
        def scores(j, last):
            kb = k_ref[pl.ds(pl.multiple_of(j * tk, tk), tk), :]
            masks = mask_of(j, last)
            return [jnp.where(masks[h % 2], _dot_nt(kb, qr_h[h // 2][h % 2]), NEG) for h in range(n_heads)]

        def max_step(j, last, ms):
            s = scores(j, last)
            return tuple(jnp.maximum(ms[h], jnp.max(s[h], axis=0, keepdims=True)) for h in range(n_heads))

        ms = lax.fori_loop(j_lo, i, lambda j, c: max_step(j, False, c),
                           tuple(jnp.full((1, tq), NEG, F32) for _ in range(n_heads)))
        ms = max_step(i, True, ms)
        acc_ref[...] = jnp.zeros_like(acc_ref)

        def sum_step(j, last, ls):
            s = scores(j, last)
            vth = vhalves(vt_ref[j])
            ls = list(ls)
            for r in range(NSA_GROUP):
                contrib = None
                for hf in range(2):
                    h = 2 * r + hf
                    p = jnp.exp2(s[h] - ms[h])
                    ls[h] = ls[h] + jnp.sum(p, axis=0, keepdims=True)
                    part = _dot(vth[hf], p.astype(BF))
                    contrib = part if contrib is None else contrib + part
                acc_ref[r] += contrib
            return tuple(ls)

        ls = lax.fori_loop(j_lo, i, lambda j, c: sum_step(j, False, c),
                           tuple(jnp.zeros((1, tq), F32) for _ in range(n_heads)))
        ls = sum_step(i, True, ls)
        return [acc_ref[r] * by_row_half(1.0 / ls[2 * r], 1.0 / ls[2 * r + 1]) for r in range(NSA_GROUP)]

    def sel_mask(j, last):
        masks = []
        for hf in range(2):
            selv = jnp.concatenate(
                [jnp.broadcast_to(sel_ref[hf, pl.ds(j * n_sub + u, 1), :], (SEL_BLOCK, tq)) for u in range(n_sub)],
                axis=0)
            mk = selv > 0.5
            if last:
                mk = mk & ((j * tk + krow) <= qpos)
            masks.append(mk)
        return masks

    o_sel = attend(ks_ref, vst_ref, sel_mask, 0)

    def win_mask(j, last):
        kpos = j * tk + krow
        mk = (kpos <= qpos) & (kpos > qpos - WINDOW)
        return mk, mk

    o_win = attend(kw_ref, vwt_ref, win_mask, jnp.maximum(i - WINDOW // tk, 0))

    gt = _sigmoid(gate_ref[...]).T
    for r in range(NSA_GROUP):
        def gate(branch):
            c = branch * 2 * NSA_GROUP + 2 * r
            return by_row_half(gt[c:c + 1, :], gt[c + 1:c + 2, :])
        o = gate(0) * occ_ref[r] + gate(1) * o_sel[r] + gate(2) * o_win[r]
        o_refs[r][...] = o.T.astype(o_refs[r].dtype)


def _nsa_attn(qn, qr, kc_bf, vct_bf, ks_bf, vst3, kw_bf, vwt3, gates, *, batch, seq):
    m = qn[0].shape[0]
    tq = KEY_BLOCK
    nq = seq // tq
    n_cmp = seq // CMP_BLOCK
    n_sel = seq // SEL_BLOCK
    qspec = pl.BlockSpec((tq, LANES), lambda b, p, i: (b * nq + i, p))
    kspec = pl.BlockSpec((seq, LANES), lambda b, p, i: (b, p))
    vspec = pl.BlockSpec((nq, LANES, tq), lambda b, p, i: (b, p, 0))
    return pl.pallas_call(
        _nsa_attn_kernel,
        grid=(batch, KV_WIDTH // LANES, nq),
        in_specs=[qspec] * 6 + [pl.BlockSpec((n_cmp, LANES), lambda b, p, i: (b, p)),
                                pl.BlockSpec((LANES, n_cmp), lambda b, p, i: (p, b)),
                                kspec, vspec, kspec, vspec, qspec],
        out_specs=[qspec] * 3,
        out_shape=[jax.ShapeDtypeStruct((m, KV_WIDTH), BF)] * 3,
        scratch_shapes=[pltpu.VMEM((tq // LANES, n_cmp, LANES), F32), pltpu.VMEM((n_sel, tq), F32),
                        pltpu.VMEM((2, n_sel, tq), F32),
                        pltpu.VMEM((NSA_GROUP, LANES, tq), F32), pltpu.VMEM((NSA_GROUP, LANES, tq), F32)],
        compiler_params=_cp("parallel", "parallel", "arbitrary"),
        name="nsa_attn",
    )(*qn, *qr, kc_bf, vct_bf, ks_bf, vst3, kw_bf, vwt3, gates)


def _tile_rows(x, reps):
    return jnp.concatenate([x] * reps, axis=0)


def _block_diag_q(q, n_heads):
    t = q.shape[0]
    assert n_heads * t <= LANES and q.shape[1] == n_heads * HEAD_DIM
    rows = _tile_rows(q, LANES // t)
    rh = jnp.right_shift(lax.broadcasted_iota(jnp.int32, rows.shape, 0), t.bit_length() - 1)
    lh = _head_of(lax.broadcasted_iota(jnp.int32, rows.shape, 1))
    return jnp.where(rh == lh, rows, 0.0).astype(BF)


def _gather_diag(o, n_heads, t):
    lh = _head_of(lax.broadcasted_iota(jnp.int32, (t, o.shape[1]), 1))
    out = jnp.zeros((t, o.shape[1]), F32)
    for h in range(n_heads):
        out = jnp.where(lh == h, o[h * t:(h + 1) * t, :], out)
    return out


def _flat_t(ref, b=None):
    x = ref[...] if b is None else ref[b]
    return x.reshape(x.shape[0] * x.shape[1], x.shape[2]).astype(BF)


def _pad_rows(pad_ref, x):
    pad_ref[...] = jnp.zeros_like(pad_ref)
    pad_ref[0:x.shape[0], :] = x
    return pad_ref[...].astype(BF)


def _row_query(shape, t):
    return jnp.bitwise_and(lax.broadcasted_iota(jnp.int32, shape, 0), t - 1)


def _softmax_lanes(s):
    e = jnp.exp2(s - jnp.max(s, axis=1, keepdims=True))
    return e / jnp.sum(e, axis=1, keepdims=True)


def _sb_dec_kernel(pps, n_steps, pt_ref, q_ref, kn_ref, vn_ref, *rest):
    kt_refs = rest[:pps]
    vt_refs = rest[pps:2 * pps]
    ut_ref, o_ref, qbd_ref, acc_ref, carry_ref, padk_ref, padv_ref = rest[2 * pps:]
    del pt_ref
    s = pl.program_id(1)
    t = q_ref.shape[0]
    ut = ut_ref[...]

    def blocks(y, new):
        lk = _sb_log_keep(y)
        if new:
            mask = lax.broadcasted_iota(jnp.int32, y.shape, 1) < _row_query(y.shape, t)
            lk = jnp.where(mask, lk, 0.0)
        c = carry_ref[...]
        ts = []
        for j in range(y.shape[1] // LANES):
            lkj = lk[:, j * LANES:(j + 1) * LANES]
            after = _dot(lkj.astype(BF), ut)
            ts.append((lkj - y[:, j * LANES:(j + 1) * LANES]) + after + c)
            c = c + jnp.broadcast_to(after[:, 0:1] + lkj[:, 0:1], c.shape)
        carry_ref[...] = c
        a = jnp.exp2(jnp.concatenate(ts, axis=1))
        if new:
            a = jnp.where(mask, a, 0.0)
        return a.astype(BF)

    @pl.when(s == 0)
    def _():
        qbd_ref[...] = _block_diag_q(q_ref[...], N_SELF_HEADS)
        carry_ref[...] = jnp.zeros_like(carry_ref)
        kn = _pad_rows(padk_ref, kn_ref[...])
        acc_ref[...] = _dot(blocks(_dot_nt(qbd_ref[...], kn), True), _pad_rows(padv_ref, vn_ref[...]))

    kt = jnp.concatenate([_flat_t(r) for r in kt_refs], axis=1)
    vt = jnp.concatenate([_flat_t(r) for r in vt_refs], axis=1)
    acc_ref[...] += _dot_nt(blocks(_dot(qbd_ref[...], kt), False), vt)

    @pl.when(s == n_steps - 1)
    def _():
        o_ref[...] = _gather_diag(acc_ref[...], N_SELF_HEADS, t)


def _sb_decode(pt, q, k_new, v_new, pool_kt, pool_vt, ut_bf, *, layer, n_pages, pps):
    m = q.shape[0]
    t = m // (pt.shape[0] // n_pages)
    db = m // t
    n_steps = n_pages // pps
    row = pl.BlockSpec((t, SELF_WIDTH), lambda b, s, pt: (b, 0))

    def page_spec(j):
        return pl.BlockSpec((None, None, N_SELF_HEADS, HEAD_DIM, PAGE_SIZE),
                            lambda b, s, pt: (layer, pt[b * n_pages + n_pages - 1 - (s * pps + j)], 0, 0, 0))

    pages = [page_spec(j) for j in range(pps)]
    return pl.pallas_call(
        functools.partial(_sb_dec_kernel, pps, n_steps),
        grid_spec=pltpu.PrefetchScalarGridSpec(
            num_scalar_prefetch=1,
            grid=(db, n_steps),
            in_specs=[row, row, row] + pages + pages + [pl.BlockSpec((LANES, LANES), lambda b, s, pt: (0, 0))],
            out_specs=row,
            scratch_shapes=[pltpu.VMEM((LANES, SELF_WIDTH), BF), pltpu.VMEM((LANES, SELF_WIDTH), F32),
                            pltpu.VMEM((LANES, LANES), F32), pltpu.VMEM((LANES, SELF_WIDTH), F32),
                            pltpu.VMEM((LANES, SELF_WIDTH), F32)]),
        out_shape=jax.ShapeDtypeStruct((m, SELF_WIDTH), F32),
        compiler_params=_cp("parallel", "arbitrary"),
        name="sb_decode",
    )(pt, q, k_new, v_new, *([pool_kt] * pps), *([pool_vt] * pps), ut_bf)


def _mem_dec_kernel(q_ref, kt_ref, vt_ref, o_ref):
    n_seq = kt_ref.shape[0]
    t = q_ref.shape[0] // n_seq
    for b in range(n_seq):
        qbd = _block_diag_q(q_ref[b * t:(b + 1) * t, :], N_MEM_HEADS)
        p = _softmax_lanes(_dot(qbd, _flat_t(kt_ref, b)))
        o_ref[b * t:(b + 1) * t, :] = _gather_diag(_dot_nt(p.astype(BF), _flat_t(vt_ref, b)), N_MEM_HEADS, t)


def _mem_decode(qm, cache_kt, cache_vt, *, layer, t):
    m = qm.shape[0]
    n_seq = math.gcd(m // t, MEM_SEQS_PER_STEP)
    row = pl.BlockSpec((n_seq * t, MEM_WIDTH), lambda b: (b, 0))
    cache = pl.BlockSpec((None, n_seq, N_MEM_HEADS, HEAD_DIM, N_MEM), lambda b: (layer, b, 0, 0, 0))
    return pl.pallas_call(
        _mem_dec_kernel,
        grid=(m // (n_seq * t),),
        in_specs=[row, cache, cache],
        out_specs=row,
        out_shape=jax.ShapeDtypeStruct((m, MEM_WIDTH), F32),
        compiler_params=_cp("parallel"),
        name="mem_decode",
    )(qm, cache_kt, cache_vt)


def _cmp_dec_kernel(n_pages, pt_ref, *refs):
    kp = refs[:n_pages]
    vp = refs[n_pages:2 * n_pages]
    pek_ref, pev_ref, wk_ref, wv_ref, ok_ref, ov_ref, xa_ref, xb_ref = refs[2 * n_pages:]
    del pt_ref
    rows = kp[0].shape[2]
    for pages, pe_ref, w_ref, o_ref in ((kp, pek_ref, wk_ref, ok_ref), (vp, pev_ref, wv_ref, ov_ref)):
        for p in range(n_pages):
            x = pages[p][...]
            xt = x.reshape(KV_WIDTH, rows).T
            xa_ref[p * rows:(p + 1) * rows, :] = xt[:, :LANES]
            xb_ref[p * rows:(p + 1) * rows, :] = xt[:, LANES:]
        nblk = o_ref.shape[0]
        xg = [pltpu.einshape("bld->lbd", r[...].reshape(nblk, CMP_BLOCK, LANES)) for r in (xa_ref, xb_ref)]
        acc = None
        for l in range(CMP_BLOCK):
            xl = jnp.concatenate([xg[0][l], xg[1][l]], axis=0) + pe_ref[l:l + 1, :]
            part = _dot(xl.astype(BF), w_ref[l])
            acc = part if acc is None else acc + part
        o_ref[...] = jnp.concatenate([acc[:nblk], acc[nblk:]], axis=1).astype(o_ref.dtype)


def _cmp_decode(pt, pool_kt, pool_vt, pek2, pev2, wk2, wv2, *, layer, n_pages, db):
    n_cmp = n_pages * PAGE_SIZE // CMP_BLOCK

    def page_spec(p):
        return pl.BlockSpec((None, None, NSA_KV_HEADS, HEAD_DIM, PAGE_SIZE),
                            lambda b, pt: (layer, pt[b * n_pages + p], 0, 0, 0))

    pages = [page_spec(p) for p in range(n_pages)]
    const2 = lambda b, pt: (0, 0)
    const3 = lambda b, pt: (0, 0, 0)
    out = pl.BlockSpec((None, n_cmp, KV_WIDTH), lambda b, pt: (b, 0, 0))
    return pl.pallas_call(
        functools.partial(_cmp_dec_kernel, n_pages),
        grid_spec=pltpu.PrefetchScalarGridSpec(
            num_scalar_prefetch=1,
            grid=(db,),
            in_specs=pages + pages + [pl.BlockSpec(pek2.shape, const2), pl.BlockSpec(pev2.shape, const2),
                                      pl.BlockSpec(wk2.shape, const3), pl.BlockSpec(wv2.shape, const3)],
            out_specs=[out, out],
            scratch_shapes=[pltpu.VMEM((n_pages * PAGE_SIZE, LANES), F32)] * 2),
        out_shape=[jax.ShapeDtypeStruct((db, n_cmp, KV_WIDTH), BF)] * 2,
        compiler_params=_cp("parallel"),
        name="cmp_decode",
    )(pt, *([pool_kt] * n_pages), *([pool_vt] * n_pages), pek2, pev2, wk2, wv2)


def _nsa_dec_kernel(n_pages, past, pt_ref, qn0, qn1, qn2, qr0, qr1, qr2, gate_ref, kc_ref, vc_ref,
                    ksn_ref, vsn_ref, kwn_ref, vwn_ref, wkt_ref, wvt_ref, pm_ref, *rest):
    kp = rest[:n_pages]
    vp = rest[n_pages:2 * n_pages]
    o0, o1, o2, score_ref, pada_ref, padb_ref = rest[2 * n_pages:]
    del pt_ref
    t = qn0.shape[0]
    t_bits = t.bit_length() - 1
    grp = NSA_KV_HEADS * t

    def build(q_refs):
        parts = []
        for r in range(NSA_GROUP):
            rows = _tile_rows(q_refs[r][...], NSA_KV_HEADS)
            rg = jnp.right_shift(lax.broadcasted_iota(jnp.int32, rows.shape, 0), t_bits)
            lg = _head_of(lax.broadcasted_iota(jnp.int32, rows.shape, 1))
            parts.append(jnp.where(rg == lg, rows, 0.0))
        parts.append(jnp.zeros((LANES - N_SELF_HEADS * t, KV_WIDTH), F32))
        return jnp.concatenate(parts, axis=0).astype(BF)

    qbn = build((qn0, qn1, qn2))
    qbr = build((qr0, qr1, qr2))

    def qpos_of(shape):
        return past + _row_query(shape, t)

    kc = kc_ref[...]
    n_cmp = kc.shape[0]
    shape_c = (LANES, n_cmp)
    c_mask = (lax.broadcasted_iota(jnp.int32, shape_c, 1) * CMP_BLOCK + (CMP_BLOCK - 1)) <= qpos_of(shape_c)
    p = _softmax_lanes(jnp.where(c_mask, _dot_nt(qbn, kc), NEG))
    live = lax.broadcasted_iota(jnp.int32, shape_c, 0) < N_SELF_HEADS * t
    p = jnp.where(c_mask & live, p, 0.0)
    o_cmp = _dot(p.astype(BF), vc_ref[...])

    imp = p[0:grp] + p[grp:2 * grp] + p[2 * grp:3 * grp]
    imp = _tile_rows(imp, LANES // grp)
    hi = imp.astype(BF)
    mid = (imp - hi.astype(F32)).astype(BF)
    lo = (imp - hi.astype(F32) - mid.astype(F32)).astype(BF)
    pm = pm_ref[...]
    blk_imp = _dot(hi, pm) + _dot(mid, pm) + _dot(lo, pm)
    blk = lax.broadcasted_iota(jnp.int32, (LANES, LANES), 1)
    cur_r = _head_of(qpos_of((LANES, LANES)))
    forced = (blk == 0) | (blk == cur_r) | (blk == cur_r - 1)
    score = jnp.where(forced, jnp.inf, jnp.where(blk <= cur_r, blk_imp, -jnp.inf))
    n_pad = score_ref.shape[0]
    score_ref[...] = score.T[0:n_pad, :]
    cur_l = _head_of(past + jnp.bitwise_and(lax.broadcasted_iota(jnp.int32, (1, LANES), 1), t - 1))
    sel_t = _topk_select(score_ref, -(-(past + t) // SEL_BLOCK), cur_l)
    sel = jnp.concatenate([sel_t, jnp.zeros((LANES - n_pad, LANES), F32)], axis=0).T

    kcol = lax.broadcasted_iota(jnp.int32, (LANES, LANES), 1)
    tq = _row_query((LANES, LANES), t)
    new_ok = (kcol <= tq) & (kcol < t)
    per_page = PAGE_SIZE // SEL_BLOCK

    n_old = n_pages * PAGE_SIZE
    new_blk = n_pages * per_page
    kt_all = jnp.concatenate([_flat_t(r) for r in kp], axis=1)
    chosen = jnp.concatenate([jnp.broadcast_to(sel[:, b:b + 1], (LANES, SEL_BLOCK)) for b in range(new_blk)],
                             axis=1) > 0.5
    s_old = jnp.where(chosen, _dot(qbr, kt_all), NEG)
    ok = (jnp.broadcast_to(sel[:, new_blk:new_blk + 1], (LANES, LANES)) > 0.5) & new_ok
    s_new = jnp.where(ok, _dot_nt(qbr, _pad_rows(pada_ref, ksn_ref[...])), NEG)
    ps = _softmax_lanes(jnp.concatenate([s_old, s_new], axis=1))
    vt_all = jnp.concatenate([_flat_t(r) for r in vp], axis=1)
    o_sel = (_dot_nt(ps[:, :n_old].astype(BF), vt_all)
             + _dot(ps[:, n_old:].astype(BF), _pad_rows(padb_ref, vsn_ref[...])))

    n_win = wkt_ref.shape[2]
    shape_w = (LANES, n_win)
    wpos = (past - n_win) + lax.broadcasted_iota(jnp.int32, shape_w, 1)
    qpos_w = qpos_of(shape_w)
    s_old = jnp.where((wpos <= qpos_w) & (wpos > qpos_w - WINDOW), _dot(qbr, _flat_t(wkt_ref)), NEG)
    s_new = jnp.where(new_ok, _dot_nt(qbr, _pad_rows(pada_ref, kwn_ref[...])), NEG)
    pw = _softmax_lanes(jnp.concatenate([s_old, s_new], axis=1))
    o_win = (_dot_nt(pw[:, :n_win].astype(BF), _flat_t(wvt_ref))
             + _dot(pw[:, n_win:].astype(BF), _pad_rows(padb_ref, vwn_ref[...])))

    gates = _tile_rows(_sigmoid(gate_ref[...]), LANES // t)
    row = lax.broadcasted_iota(jnp.int32, gates.shape, 0)
    col = lax.broadcasted_iota(jnp.int32, gates.shape, 1)
    r_of = jnp.right_shift(row, t_bits + 2)
    g_of = jnp.bitwise_and(jnp.right_shift(row, t_bits), NSA_KV_HEADS - 1)
    base = jnp.right_shift(g_of, 1) * LANES + r_of * 2 + jnp.bitwise_and(g_of, 1)

    def gate(branch):
        return jnp.sum(jnp.where(col == base + branch * 2 * NSA_GROUP, gates, 0.0), axis=1, keepdims=True)

    o = gate(0) * o_cmp + gate(1) * o_sel + gate(2) * o_win
    for r, o_ref in enumerate((o0, o1, o2)):
        o_ref[...] = _gather_diag(o[r * grp:(r + 1) * grp, :], NSA_KV_HEADS, t)


def _nsa_decode(pt, qn, qr, gates, kc, vc, new_rows, win_kt, win_vt, pool_kt, pool_vt, pair_mat, *, layer, n_pages, t):
    m = qn[0].shape[0]
    db = m // t
    past = n_pages * PAGE_SIZE
    n_cmp = kc.shape[1]
    n_win = win_kt.shape[4]
    n_sel_pad = -(-(-(-(past + t) // SEL_BLOCK)) // 8) * 8
    row = pl.BlockSpec((t, KV_WIDTH), lambda b, pt: (b, 0))
    cmp_spec = pl.BlockSpec((None, n_cmp, KV_WIDTH), lambda b, pt: (b, 0, 0))
    win_spec = pl.BlockSpec((None, None, NSA_KV_HEADS, HEAD_DIM, n_win), lambda b, pt: (layer, b, 0, 0, 0))

    def page_spec(p):
        return pl.BlockSpec((None, None, NSA_KV_HEADS, HEAD_DIM, PAGE_SIZE),
                            lambda b, pt: (layer, pt[b * n_pages + p], 0, 0, 0))

    pages = [page_spec(p) for p in range(n_pages)]
    return pl.pallas_call(
        functools.partial(_nsa_dec_kernel, n_pages, past),
        grid_spec=pltpu.PrefetchScalarGridSpec(
            num_scalar_prefetch=1,
            grid=(db,),
            in_specs=[row] * 7 + [cmp_spec, cmp_spec] + [row] * 4 + [win_spec, win_spec]
            + [pl.BlockSpec(pair_mat.shape, lambda b, pt: (0, 0))] + pages + pages,
            out_specs=[row] * 3,
            scratch_shapes=[pltpu.VMEM((n_sel_pad, LANES), F32),
                            pltpu.VMEM((LANES, KV_WIDTH), F32), pltpu.VMEM((LANES, KV_WIDTH), F32)]),
        out_shape=[jax.ShapeDtypeStruct((m, KV_WIDTH), F32)] * 3,
        compiler_params=_cp("parallel"),
        name="nsa_decode",
    )(pt, *qn, *qr, gates, kc, vc, *new_rows, win_kt, win_vt, pair_mat,
      *([pool_kt] * n_pages), *([pool_vt] * n_pages))


NSA_IN = SELF_WIDTH + 6 * KV_WIDTH + 3 * N_SELF_HEADS + MEM_WIDTH
ROW_TILE = 512
SB_PAGES_PER_STEP = 16
MEM_SEQS_PER_STEP = 4


def _nsa_head_order():
    return [NSA_GROUP * g + r for r in range(NSA_GROUP) for g in range(NSA_KV_HEADS)]


def _nsa_in_columns():
    q = [h * HEAD_DIM + d for h in _nsa_head_order() for d in range(HEAD_DIM)]
    kv = list(range(SELF_WIDTH, SELF_WIDTH + 6 * KV_WIDTH))
    g0 = SELF_WIDTH + 6 * KV_WIDTH
    gate = []
    for gp in range(KV_WIDTH // LANES):
        for c in range(LANES):
            if c < 3 * 2 * NSA_GROUP:
                branch, r, hf = c // (2 * NSA_GROUP), (c % (2 * NSA_GROUP)) // 2, c % 2
                gate.append(g0 + (NSA_GROUP * (2 * gp + hf) + r) * 3 + branch)
            else:
                gate.append(NSA_IN)
    qm = list(range(g0 + 3 * N_SELF_HEADS, NSA_IN))
    cols = np.asarray(q + kv + gate + qm, np.int32)
    assert cols.shape[0] == _NSA_COLS
    return cols


def _nsa_out_rows():
    o = [h * HEAD_DIM + d for h in _nsa_head_order() for d in range(HEAD_DIM)]
    return np.asarray(o + list(range(SELF_WIDTH, SELF_WIDTH + MEM_WIDTH)), np.int32)


def _block_diag2(w):
    eye = jnp.eye(LANES // HEAD_DIM, dtype=w.dtype)
    return jnp.einsum("gh,lde->lgdhe", eye, w).reshape(w.shape[0], LANES, LANES).astype(BF)


def kernel(x_prompt, x_sample, mem_prompt, cache_sb_k, cache_sb_v, cache_nsa_cmp_k, cache_nsa_cmp_v, cache_nsa_sel_k, cache_nsa_sel_v, cache_nsa_win_k, cache_nsa_win_v, cache_mem_k, cache_mem_v, page_table, ln_mix_pre, ln_mix_post, ln_ffn_pre, ln_ffn_post, ln_mem, w_in_a, w_in_b, w_cmp_k, w_cmp_v, pe_cmp_k, pe_cmp_v, w_out, w_mem_kv, w_gate_up, w_down):
    bsz, seq, d = x_prompt.shape
    db, ds, _ = x_sample.shape
    n_pages = page_table.shape[1]
    past = n_pages * PAGE_SIZE
    depth = w_out.shape[0]
    n_win = cache_nsa_win_k.shape[2]
    tm_p = min(ROW_TILE, bsz * seq)
    tm_s = min(ROW_TILE, db * ds)
    assert d == D_MODEL and seq % KEY_BLOCK == 0 and seq % tm_p == 0 and (db * ds) % tm_s == 0 and tm_s % ds == 0
    assert ds & (ds - 1) == 0 and N_SELF_HEADS * ds <= LANES and ds < CMP_BLOCK
    assert past % SEL_BLOCK == 0 and past + ds <= past + SEL_BLOCK and n_win % LANES == 0 and n_win <= past
    assert n_pages % SB_PAGES_PER_STEP == 0 and seq >= WINDOW

    h_p = x_prompt.reshape(bsz * seq, d)
    h_s = x_sample.reshape(db * ds, d)
    mem = mem_prompt.reshape(bsz * N_MEM, d)
    pt = page_table.reshape(-1).astype(jnp.int32)
    u256 = jnp.triu(jnp.ones((KEY_BLOCK, KEY_BLOCK), BF), 1)
    ut128 = jnp.tril(jnp.ones((LANES, LANES), BF), -1)
    pair_np = np.zeros((past // CMP_BLOCK, LANES), np.float32)
    pair_np[np.arange(past // CMP_BLOCK), np.arange(past // CMP_BLOCK) // (SEL_BLOCK // CMP_BLOCK)] = 1.0
    pair_mat = jnp.asarray(pair_np, BF)
    tmin = lambda c: jnp.transpose(c, (0, 1, 3, 4, 2))
    sbk_t, sbv_t = tmin(cache_sb_k), tmin(cache_sb_v)
    cmpk_t, cmpv_t = tmin(cache_nsa_cmp_k), tmin(cache_nsa_cmp_v)
    selk_t, selv_t = tmin(cache_nsa_sel_k), tmin(cache_nsa_sel_v)
    wink_t, winv_t = tmin(cache_nsa_win_k), tmin(cache_nsa_win_v)
    memk_t, memv_t = tmin(cache_mem_k), tmin(cache_mem_v)
    rope_p = _rope_tables(jnp.arange(seq, dtype=jnp.int32))
    rope_s = _rope_tables(past + jnp.arange(tm_s, dtype=jnp.int32) % ds)

    nsa_cols = _nsa_in_columns()
    nsa_rows = _nsa_out_rows()

    sb_p, sb_s = [[], []], [[], []]
    nsa_p, nsa_s = [[] for _ in range(6)], [[] for _ in range(6)]
    memk_p, memv_p = [], []
    for i in range(depth):
        j = i // 2
        g_pre = ln_mix_pre[i][None]
        tail_w = (ln_mix_post[i][None], ln_ffn_pre[i][None], ln_ffn_post[i][None],
                  w_gate_up[i].astype(BF), w_down[i].astype(BF))
        wkv = w_mem_kv[i]
        mk_bf, mkt, mvt, mvt_bf = _memkv(mem, ln_mem[i][None], wkv.astype(BF), wkv.T.astype(BF))
        memk_p.append(mkt)
        memv_p.append(mvt)
        if i % 2 == 0:
            w = w_in_a[j]
            w_bf = w.astype(BF)
            wt = w[:, SELF_WIDTH:3 * SELF_WIDTH].T.astype(BF)
            q, qm, kb, kt, vt, vt3 = _sb_inproj(h_p, g_pre, w_bf, wt, tm=tm_p, seq=seq)
            o = _sb_attn(q, kb, vt3, u256, batch=bsz, seq=seq)
            om = _mem_attn(qm, mk_bf, mvt_bf, batch=bsz, seq=seq)
            qs, qms, ks, vs = _sb_inproj(h_s, g_pre, w_bf, wt, tm=tm_s, seq=None)
            o_s = _sb_decode(pt, qs, ks, vs, sbk_t, sbv_t, ut128, layer=j, n_pages=n_pages,
                             pps=SB_PAGES_PER_STEP)
            om_s = _mem_decode(qms, memk_t, memv_t, layer=i, t=ds)
            wo = w_out[i].astype(BF)
            mix_p, mix_s = [o, om], [o_s, om_s]
            for n, (rows_t, rows) in enumerate(((kt, ks), (vt, vs))):
                sb_p[n].append(rows_t)
                sb_s[n].append(rows.reshape(db, ds, N_SELF_HEADS, HEAD_DIM))
        else:
            w = w_in_b[j]
            w_bf = jnp.concatenate([w, jnp.zeros((d, 1), w.dtype)], axis=1)[:, nsa_cols].astype(BF)
            kv_cols = lambda n: w[:, SELF_WIDTH + n * KV_WIDTH:SELF_WIDTH + (n + 1) * KV_WIDTH]
            wt = jnp.concatenate([kv_cols(0), kv_cols(1), kv_cols(3), kv_cols(5)], axis=1).T.astype(BF)
            pek2 = jnp.tile(pe_cmp_k[j], (1, LANES // HEAD_DIM))
            pev2 = jnp.tile(pe_cmp_v[j], (1, LANES // HEAD_DIM))
            wck2 = _block_diag2(w_cmp_k[j])
            wcv2 = _block_diag2(w_cmp_v[j])
            outs = _nsa_inproj(h_p, g_pre, w_bf, wt, *rope_p, tm=tm_p, seq=seq)
            qn, qr, gates, qm = outs[0:3], outs[3:6], outs[6], outs[7]
            kc, vc, ksb, kwb = outs[8:12]
            rows_t, (vst3, vwt3) = outs[12:18], outs[18:20]
            kc_bf, _ = _compress(kc, pek2, wck2, rows=seq)
            _, vct_bf = _compress(vc, pev2, wcv2, rows=seq)
            o3 = _nsa_attn(qn, qr, kc_bf, vct_bf, ksb, vst3, kwb, vwt3, gates, batch=bsz, seq=seq)
            om = _mem_attn(qm, mk_bf, mvt_bf, batch=bsz, seq=seq)
            outs = _nsa_inproj(h_s, g_pre, w_bf, wt, *rope_s, tm=tm_s, seq=None)
            qn_s, qr_s, gates_s, qms, rows_s = outs[0:3], outs[3:6], outs[6], outs[7], outs[8:14]
            kc_s, vc_s = _cmp_decode(pt, cmpk_t, cmpv_t, pek2, pev2, wck2, wcv2, layer=j, n_pages=n_pages, db=db)
            o3_s = _nsa_decode(pt, qn_s, qr_s, gates_s, kc_s, vc_s, rows_s[2:6], wink_t, winv_t, selk_t, selv_t,
                               pair_mat, layer=j, n_pages=n_pages, t=ds)
            om_s = _mem_decode(qms, memk_t, memv_t, layer=i, t=ds)
            wo = w_out[i][nsa_rows].astype(BF)
            mix_p, mix_s = [*o3, om], [*o3_s, om_s]
            for n in range(6):
                nsa_p[n].append(rows_t[n][:, :, seq - min(WINDOW, seq):] if n >= 4 else rows_t[n])
                nsa_s[n].append(rows_s[n].reshape(db, ds, NSA_KV_HEADS, HEAD_DIM))
        h_p = _layer_tail(h_p, mix_p, wo, *tail_w, tm=tm_p)
        h_s = _layer_tail(h_s, mix_s, wo, *tail_w, tm=tm_s)

    st = lambda rows: jnp.stack(rows, axis=0)

    def tokens_first(rows_t):
        x = st(rows_t)
        nl, nb, width, nt = x.shape
        return x.reshape(nl, nb, width // HEAD_DIM, HEAD_DIM, nt).transpose(0, 1, 4, 2, 3)

    return (h_p.reshape(bsz, seq, d), h_s.reshape(db, ds, d),
            tokens_first(sb_p[0]), tokens_first(sb_p[1]),
            *[tokens_first(x) for x in nsa_p],
            tokens_first(memk_p), tokens_first(memv_p),
            st(sb_s[0]), st(sb_s[1]),
            *[st(x) for x in nsa_s])
```
